```python
import math
import jax, jax.numpy as jnp
from jax import lax
import numpy as np

D_MODEL = 2048
BATCH = 16
SEQ = 2048
DEPTH = 2

MEM_LEN = 256
RET_HEADS = 4
RET_WIDTH = D_MODEL // 4
RET_DV = RET_WIDTH // RET_HEADS
RET_DK = RET_DV
RET_CHUNK = 128
S5_WIDTH = D_MODEL // 4
S5_GROUP = 16
S5_GROUPS = S5_WIDTH // S5_GROUP
S5_STATE = 64
DIFF_WIDTH = D_MODEL // 2
DIFF_HEADS = 8
DIFF_DV = DIFF_WIDTH // DIFF_HEADS
DIFF_DQK = DIFF_DV // 2
Q_BLOCK = 128
REL_BUCKETS = 32
REL_MAX_DIST = 128
X_HEADS = 4
X_HEAD_DIM = 128
X_WIDTH = X_HEADS * X_HEAD_DIM
D_FF = 4 * D_MODEL
EPS = 1e-6

MIX_WIDTH = RET_WIDTH + S5_WIDTH + DIFF_WIDTH
RET_QK_W = RET_HEADS * RET_DK
DIFF_QK_W = DIFF_HEADS * 2 * DIFF_DQK
IN_SIZES = (RET_QK_W, RET_QK_W, RET_WIDTH, RET_WIDTH, S5_WIDTH, DIFF_QK_W, DIFF_QK_W, DIFF_WIDTH)
IN_WIDTH = int(sum(IN_SIZES))
IN_SPLITS = tuple(int(v) for v in np.cumsum(IN_SIZES)[:-1])

kernel_name = 'hybrid_ret_s5_diffattn_block'


def rmsnorm(x, g):
    xf = x.astype(jnp.float32)
    y = xf * lax.rsqrt(jnp.mean(xf * xf, axis=-1, keepdims=True) + EPS)
    return (y * g.astype(jnp.float32)).astype(x.dtype)


def rotate(x, pos):
    half = x.shape[-1] // 2
    inv = 10000.0 ** (-jnp.arange(half, dtype=jnp.float32) / half)
    ang = pos.astype(jnp.float32)[..., None] * inv
    cos = jnp.cos(ang)[:, :, None, :]
    sin = jnp.sin(ang)[:, :, None, :]
    xf = x.astype(jnp.float32)
    x1, x2 = xf[..., :half], xf[..., half:]
    return jnp.concatenate([x1 * cos - x2 * sin, x1 * sin + x2 * cos], axis=-1)


def retention_group(q, k, v, gate, pos):
    B, S = q.shape[:2]
    H, C = RET_HEADS, RET_CHUNK
    n = S // C
    q = rotate(q.reshape(B, S, H, RET_DK), pos)
    k = rotate(k.reshape(B, S, H, RET_DK), pos) * (RET_DK ** -0.5)
    v = v.reshape(B, S, H, RET_DV).astype(jnp.float32)

    def chunks(t):
        return jnp.moveaxis(t.reshape(B, n, C, H, t.shape[-1]), 1, 0)

    log_g = jnp.log(1.0 - 2.0 ** (-5.0 - jnp.arange(H, dtype=jnp.float32)))
    idx = jnp.arange(C, dtype=jnp.float32)
    dist = idx[:, None] - idx[None, :]
    decay = jnp.where(dist >= 0, jnp.exp(jnp.maximum(dist, 0.0)[None] * log_g[:, None, None]), 0.0)
    xi = jnp.exp((idx + 1.0)[:, None] * log_g[None, :])
    zeta = jnp.exp((C - 1.0 - idx)[:, None] * log_g[None, :])
    g_chunk = jnp.exp(C * log_g)

    def step(R, qkv):
        qc, kc, vc = qkv
        inner = jnp.einsum('bchd,bshd->bhcs', qc, kc) * decay
        o = jnp.einsum('bhcs,bshe->bche', inner, vc)
        o = o + jnp.einsum('bchd,bhde->bche', qc * xi[None, :, :, None], R)
        R = g_chunk[None, :, None, None] * R + jnp.einsum('bshd,bshe->bhde', kc * zeta[None, :, :, None], vc)
        return R, o

    R0 = jnp.zeros((B, H, RET_DK, RET_DV), jnp.float32)
    _, o = lax.scan(step, R0, (chunks(q), chunks(k), chunks(v)))
    o = jnp.moveaxis(o, 0, 1).reshape(B, S, H, RET_DV)
    o = o * lax.rsqrt(jnp.mean(o * o, axis=-1, keepdims=True) + EPS)
    return o.reshape(B, S, RET_WIDTH) * jax.nn.silu(gate.astype(jnp.float32))


def _ssm_combine(e1, e2):
    a1r, a1i, b1r, b1i = e1
    a2r, a2i, b2r, b2i = e2
    ar = a1r * a2r - a1i * a2i
    ai = a1r * a2i + a1i * a2r
    br = a2r * b1r - a2i * b1i + b2r
    bi = a2r * b1i + a2i * b1r + b2i
    return (ar, ai, br, bi)


def s5_group(u, lam_re, lam_im, log_dt, b_re, b_im, c_re, c_im, d_skip, w_glu):
    B, S = u.shape[:2]
    G, P = S5_GROUPS, S5_STATE
    uf = u.astype(jnp.float32).reshape(B, S, G, S5_GROUP)
    lr = jnp.minimum(lam_re.astype(jnp.float32), -1e-4)
    li = lam_im.astype(jnp.float32)
    dt = jnp.exp(log_dt.astype(jnp.float32))[:, None]
    mag = jnp.exp(lr * dt)
    ab_re = mag * jnp.cos(li * dt)
    ab_im = mag * jnp.sin(li * dt)
    nr, ni = ab_re - 1.0, ab_im
    den = lr * lr + li * li
    f_re = ((nr * lr + ni * li) / den)[..., None]
    f_im = ((ni * lr - nr * li) / den)[..., None]
    br, bi = b_re.astype(jnp.float32), b_im.astype(jnp.float32)
    bb_re = f_re * br - f_im * bi
    bb_im = f_re * bi + f_im * br
    bu_re = jnp.einsum('bsgh,gph->sbgp', uf, bb_re)
    bu_im = jnp.einsum('bsgh,gph->sbgp', uf, bb_im)
    a_re = jnp.broadcast_to(ab_re, (S, 1, G, P))
    a_im = jnp.broadcast_to(ab_im, (S, 1, G, P))
    _, _, x_re, x_im = lax.associative_scan(_ssm_combine, (a_re, a_im, bu_re, bu_im), axis=0)
    y = (jnp.einsum('sbgp,ghp->bsgh', x_re, c_re.astype(jnp.float32))
         - jnp.einsum('sbgp,ghp->bsgh', x_im, c_im.astype(jnp.float32)))
    y = y.reshape(B, S, S5_WIDTH) + d_skip.astype(jnp.float32) * uf.reshape(B, S, S5_WIDTH)
    g = jax.nn.gelu(y)
    return g * jax.nn.sigmoid(g @ w_glu.astype(jnp.float32))


def t5_bias(qpos, kpos, table):
    n = jnp.maximum(qpos[:, :, None] - kpos[:, None, :], 0)
    max_exact = REL_BUCKETS // 2
    large = max_exact + (jnp.log(jnp.maximum(n, 1).astype(jnp.float32) / max_exact)
                         / math.log(REL_MAX_DIST / max_exact) * (REL_BUCKETS - max_exact)).astype(jnp.int32)
    large = jnp.minimum(large, REL_BUCKETS - 1)
    bucket = jnp.where(n < max_exact, n, large)
    return jnp.moveaxis(table.astype(jnp.float32)[bucket], -1, 1)


def _block_probs(qa, ka, bias, causal, scale):
    s = jnp.einsum('bqhd,bkhd->bhqk', qa, ka).astype(jnp.float32) * scale + bias
    return jax.nn.softmax(jnp.where(causal, s, -jnp.inf), axis=-1)


def diff_group(q, k, v, pos, rel_bias, lam, lam_init, g_subln):
    B, S = q.shape[:2]
    H = DIFF_HEADS
    q = q.reshape(B, S, H, 2, DIFF_DQK)
    k = k.reshape(B, S, H, 2, DIFF_DQK)
    v = v.reshape(B, S, H, DIFF_DV).astype(jnp.float32)
    q1, q2 = q[..., 0, :], q[..., 1, :]
    k1, k2 = k[..., 0, :], k[..., 1, :]
    scale = DIFF_DQK ** -0.5
    outs = []
    for i in range(S // Q_BLOCK):
        q0 = i * Q_BLOCK
        kend = q0 + Q_BLOCK
        bias = t5_bias(pos[:, q0:kend], pos[:, :kend], rel_bias)
        causal = jnp.arange(kend)[None, :] <= (q0 + jnp.arange(Q_BLOCK))[:, None]
        p1 = _block_probs(q1[:, q0:kend], k1[:, :kend], bias, causal, scale)
        p2 = _block_probs(q2[:, q0:kend], k2[:, :kend], bias, causal, scale)
        outs.append(jnp.einsum('bhqk,bkhe->bqhe', p1 - lam * p2, v[:, :kend]))
    o = jnp.concatenate(outs, axis=1)
    o = rmsnorm(o, g_subln) * (1.0 - lam_init)
    return o.reshape(B, S, DIFF_WIDTH)


def cross_attention(h, mem_n, w_q, w_kv, w_o):
    B, S = h.shape[:2]
    M = mem_n.shape[1]
    q = (h @ w_q).reshape(B, S, X_HEADS, X_HEAD_DIM)
    k, v = jnp.split(mem_n @ w_kv, 2, axis=-1)
    k = k.reshape(B, M, X_HEADS, X_HEAD_DIM)
    v = v.reshape(B, M, X_HEADS, X_HEAD_DIM)
    s = jnp.einsum('bshd,bmhd->bhsm', q, k).astype(jnp.float32) * (X_HEAD_DIM ** -0.5)
    p = jax.nn.softmax(s, axis=-1).astype(v.dtype)
    o = jnp.einsum('bhsm,bmhd->bshd', p, v).reshape(B, S, X_WIDTH)
    return o @ w_o


def setup_inputs(seed: int = 0) -> dict:
    key = jax.random.key(seed)
    ks = jax.random.split(key, 32)
    f32 = jnp.float32
    nrm = lambda k, shape, s: jax.random.normal(k, shape, f32) * s
    gain = lambda k, shape: 1.0 + 0.02 * jax.random.normal(k, shape, f32)
    offs = jax.random.randint(ks[2], (BATCH, 1), 0, 4096, dtype=jnp.int32)
    positions = offs + jnp.arange(SEQ, dtype=jnp.int32)[None, :]
    lam_im0 = math.pi * jnp.arange(S5_STATE, dtype=f32)
    return {
        'x': nrm(ks[0], (BATCH, SEQ, D_MODEL), 1.0),
        'mem': nrm(ks[1], (BATCH, MEM_LEN, D_MODEL), 1.0),
        'positions': positions,
        'rel_bias': nrm(ks[3], (REL_BUCKETS, DIFF_HEADS), 0.5),
        'w_in': nrm(ks[4], (DEPTH, D_MODEL, IN_WIDTH), D_MODEL ** -0.5),
        'w_out': nrm(ks[5], (DEPTH, MIX_WIDTH, D_MODEL), MIX_WIDTH ** -0.5),
        'lam_re': -0.5 + nrm(ks[6], (DEPTH, S5_GROUPS, S5_STATE), 0.01),
        'lam_im': lam_im0 + nrm(ks[7], (DEPTH, S5_GROUPS, S5_STATE), 0.01),
        'log_dt': jax.random.uniform(ks[8], (DEPTH, S5_GROUPS), f32, math.log(1e-3), math.log(1e-1)),
        'b_re': nrm(ks[9], (DEPTH, S5_GROUPS, S5_STATE, S5_GROUP), (2 * S5_GROUP) ** -0.5),
        'b_im': nrm(ks[10], (DEPTH, S5_GROUPS, S5_STATE, S5_GROUP), (2 * S5_GROUP) ** -0.5),
        'c_re': nrm(ks[11], (DEPTH, S5_GROUPS, S5_GROUP, S5_STATE), S5_STATE ** -0.5),
        'c_im': nrm(ks[12], (DEPTH, S5_GROUPS, S5_GROUP, S5_STATE), S5_STATE ** -0.5),
        'd_skip': nrm(ks[13], (DEPTH, S5_WIDTH), 1.0),
        'w_glu': nrm(ks[14], (DEPTH, S5_WIDTH, S5_WIDTH), S5_WIDTH ** -0.5),
        'lam_q1': nrm(ks[15], (DEPTH, DIFF_DQK), 0.1),
        'lam_k1': nrm(ks[16], (DEPTH, DIFF_DQK), 0.1),
        'lam_q2': nrm(ks[17], (DEPTH, DIFF_DQK), 0.1),
        'lam_k2': nrm(ks[18], (DEPTH, DIFF_DQK), 0.1),
        'g_subln': gain(ks[19], (DEPTH, DIFF_DV)),
        'w_xq': nrm(ks[20], (DEPTH, D_MODEL, X_WIDTH), D_MODEL ** -0.5),
        'w_xkv': nrm(ks[21], (DEPTH, D_MODEL, 2 * X_WIDTH), D_MODEL ** -0.5),
        'w_xo': nrm(ks[22], (DEPTH, X_WIDTH, D_MODEL), X_WIDTH ** -0.5),
        'w_up': nrm(ks[23], (DEPTH, D_MODEL, D_FF), D_MODEL ** -0.5),
        'w_down': nrm(ks[24], (DEPTH, D_FF, D_MODEL), D_FF ** -0.5),
        'g_mix_pre': gain(ks[25], (DEPTH, D_MODEL)),
        'g_mix_post': gain(ks[26], (DEPTH, D_MODEL)),
        'g_mem': gain(ks[27], (DEPTH, D_MODEL)),
        'g_x_pre': gain(ks[28], (DEPTH, D_MODEL)),
        'g_x_post': gain(ks[29], (DEPTH, D_MODEL)),
        'g_mlp_pre': gain(ks[30], (DEPTH, D_MODEL)),
        'g_mlp_post': gain(ks[31], (DEPTH, D_MODEL)),
    }


def reference(x, mem, positions, rel_bias, w_in, w_out, lam_re, lam_im, log_dt, b_re, b_im,
              c_re, c_im, d_skip, w_glu, lam_q1, lam_k1, lam_q2, lam_k2, g_subln,
              w_xq, w_xkv, w_xo, w_up, w_down, g_mix_pre, g_mix_post, g_mem,
              g_x_pre, g_x_post, g_mlp_pre, g_mlp_post):
    for l in range(DEPTH):
        lam_init = 0.8 - 0.6 * math.exp(-0.3 * l)
        h = rmsnorm(x, g_mix_pre[l])
        proj = h @ w_in[l]
        rq, rk, rv, rg, su, dq, dk, dv = jnp.split(proj, IN_SPLITS, axis=-1)
        y_ret = retention_group(rq, rk, rv, rg, positions)
        y_s5 = s5_group(su, lam_re[l], lam_im[l], log_dt[l], b_re[l], b_im[l],
                        c_re[l], c_im[l], d_skip[l], w_glu[l])
        lam = (jnp.exp(jnp.sum(lam_q1[l].astype(jnp.float32) * lam_k1[l].astype(jnp.float32)))
               - jnp.exp(jnp.sum(lam_q2[l].astype(jnp.float32) * lam_k2[l].astype(jnp.float32)))
               + lam_init)
        y_diff = diff_group(dq, dk, dv, positions, rel_bias, lam, lam_init, g_subln[l])
        mixed = jnp.concatenate([y_ret, y_s5, y_diff], axis=-1).astype(x.dtype) @ w_out[l]
        x = x + rmsnorm(mixed, g_mix_post[l])
        h = rmsnorm(x, g_x_pre[l])
        mem_n = rmsnorm(mem, g_mem[l])
        x = x + rmsnorm(cross_attention(h, mem_n, w_xq[l], w_xkv[l], w_xo[l]), g_x_post[l])
        h = rmsnorm(x, g_mlp_pre[l])
        y = jnp.square(jax.nn.relu(h @ w_up[l])) @ w_down[l]
        x = x + rmsnorm(y, g_mlp_post[l])
    return x
```

```python
import functools
import math

import jax
import jax.numpy as jnp
from jax import lax
from jax.experimental import pallas as pl
from jax.experimental.pallas import tpu as pltpu

F32 = jnp.float32
BF16 = jnp.bfloat16

D_MODEL = 2048
DEPTH = 2
MEM_LEN = 256
RET_HEADS = 4
RET_DK = 128
RET_CHUNK = 128
S5_WIDTH = 512
S5_GROUP = 16
S5_GROUPS = 32
S5_STATE = 64
S5_NSTATE = S5_GROUPS * S5_STATE
DIFF_HEADS = 8
DIFF_DV = 128
DIFF_DQK = 64
Q_BLOCK = 128
REL_BUCKETS = 32
REL_MAX_DIST = 128
X_HEADS = 4
X_HEAD_DIM = 128
X_WIDTH = X_HEADS * X_HEAD_DIM
D_FF = 4 * D_MODEL
EPS = 1e-6
IN_WIDTH = 5632
COL_RET_Q, COL_RET_K, COL_RET_V, COL_RET_G = 0, 4, 8, 12
COL_S5_U512 = 4
COL_DIFF_Q, COL_DIFF_K, COL_DIFF_V = 20, 28, 36

LANES = 128
SUBLANES = 8
VMEM_LIMIT = 56 * 1024 * 1024

NT = (((1,), (1,)), ((), ()))
TN = (((0,), (0,)), ((), ()))


def _cparams(*sem):
    return pltpu.CompilerParams(dimension_semantics=sem, vmem_limit_bytes=VMEM_LIMIT)


def _rms(x, g):
    ms = jnp.mean(x * x, axis=-1, keepdims=True)
    return x * lax.rsqrt(ms + EPS) * g


def _dot(a, b):
    return jnp.dot(a, b, preferred_element_type=F32)


def _inproj_kernel(x_ref, g_ref, w_ref, o_ref, h_ref):
    @pl.when(pl.program_id(1) == 0)
    def _():
        h_ref[...] = _rms(x_ref[...], g_ref[...]).astype(BF16)

    o_ref[...] = _dot(h_ref[...], w_ref[...])


def _inproj(x2, g, w):
    tok, d = x2.shape
    n = w.shape[1]
    tm = min(1024, tok)
    tn = 512
    return pl.pallas_call(
        _inproj_kernel,
        grid=(tok // tm, n // tn),
        in_specs=[
            pl.BlockSpec((tm, d), lambda i, j: (i, 0)),
            pl.BlockSpec((1, d), lambda i, j: (0, 0)),
            pl.BlockSpec((d, tn), lambda i, j: (0, j)),
        ],
        out_specs=pl.BlockSpec((tm, tn), lambda i, j: (i, j)),
        out_shape=jax.ShapeDtypeStruct((tok, n), F32),
        scratch_shapes=[pltpu.VMEM((tm, d), BF16)],
        compiler_params=_cparams("parallel", "arbitrary"),
        name="inproj",
    )(x2, g, w)


def _rot_kernel(pos_ref, cos_ref, sin_ref):
    pos = pos_ref[...].astype(F32)
    lane = lax.broadcasted_iota(jnp.int32, (1, LANES), 1)
    half = RET_DK // 2
    j = jnp.where(lane < half, lane, lane - half).astype(F32)
    inv = jnp.exp(j * (-math.log(10000.0) / half))
    ang = pos * inv
    cos_ref[...] = jnp.cos(ang)
    s = jnp.sin(ang)
    sin_ref[...] = jnp.where(lane < half, -s, s)


def _rot_tables(positions):
    b, s = positions.shape
    out = jax.ShapeDtypeStruct((b, s, LANES), F32)
    return pl.pallas_call(
        _rot_kernel,
        grid=(b,),
        in_specs=[pl.BlockSpec((None, s, 1), lambda i: (i, 0, 0))],
        out_specs=[pl.BlockSpec((None, s, LANES), lambda i: (i, 0, 0))] * 2,
        out_shape=[out, out],
        compiler_params=_cparams("parallel"),
        name="rot_tables",
    )(positions.reshape(b, s, 1))


def _ret_kernel(q_ref, k_ref, v_ref, g_ref, cos_ref, sin_ref, dec_ref, xi_ref, zeta_ref, gc_ref, o_ref):
    c_len = RET_CHUNK
    n_chunks = q_ref.shape[0] // c_len
    dec = dec_ref[...]
    xi = xi_ref[...]
    zeta = zeta_ref[...]
    gc = gc_ref[...]
    scale = RET_DK ** -0.5

    def body(c, r_state):
        sl = pl.ds(pl.multiple_of(c * c_len, c_len), c_len)
        cs = cos_ref[sl, :]
        sn = sin_ref[sl, :]
        q = q_ref[sl, :]
        k = k_ref[sl, :]
        qr = q * cs + pltpu.roll(q, RET_DK // 2, 1) * sn
        kr = (k * cs + pltpu.roll(k, RET_DK // 2, 1) * sn) * scale
        vb = v_ref[sl, :].astype(BF16)
        inner = lax.dot_general(qr.astype(BF16), kr.astype(BF16), NT, preferred_element_type=F32) * dec
        o = _dot(inner.astype(BF16), vb)
        o = o + _dot((qr * xi).astype(BF16), r_state.astype(BF16))
        r_new = gc * r_state + lax.dot_general((kr * zeta).astype(BF16), vb, TN, preferred_element_type=F32)
        o = o * lax.rsqrt(jnp.mean(o * o, axis=-1, keepdims=True) + EPS)
        gate = g_ref[sl, :]
        o_ref[sl, :] = (o * (gate * jax.nn.sigmoid(gate))).astype(o_ref.dtype)
        return r_new

    lax.fori_loop(0, n_chunks, body, jnp.zeros((RET_DK, RET_DK), F32))


def _ret_consts():
    h, c = RET_HEADS, RET_CHUNK
    log_g = jnp.log(1.0 - 2.0 ** (-5.0 - jnp.arange(h, dtype=F32)))
    idx = jnp.arange(c, dtype=F32)
    dist = idx[:, None] - idx[None, :]
    decay = jnp.where(dist >= 0, jnp.exp(jnp.maximum(dist, 0.0)[None] * log_g[:, None, None]), 0.0)
    xi = jnp.exp((idx + 1.0)[None, :] * log_g[:, None])
    zeta = jnp.exp((c - 1.0 - idx)[None, :] * log_g[:, None])
    g_chunk = jnp.exp(c * log_g)
    bc = lambda t: jnp.broadcast_to(t[:, :, None], (h, c, LANES))
    gcb = jnp.broadcast_to(g_chunk[:, None, None], (h, c, LANES))
    return decay, bc(xi), bc(zeta), gcb


def _retention(proj3, cos2, sin2):
    b, s, _ = proj3.shape
    decay, xi, zeta, gcb = _ret_consts()
    tok_blk = lambda col: pl.BlockSpec((None, s, LANES), lambda i, h: (i, 0, col + h))
    pos_blk = pl.BlockSpec((None, s, LANES), lambda i, h: (i, 0, 0))
    head_blk = pl.BlockSpec((None, RET_CHUNK, LANES), lambda i, h: (h, 0, 0))
    return pl.pallas_call(
        _ret_kernel,
        grid=(b, RET_HEADS),
        in_specs=[tok_blk(COL_RET_Q), tok_blk(COL_RET_K), tok_blk(COL_RET_V), tok_blk(COL_RET_G),
                  pos_blk, pos_blk, head_blk, head_blk, head_blk, head_blk],
        out_specs=pl.BlockSpec((None, s, LANES), lambda i, h: (i, 0, h)),
        out_shape=jax.ShapeDtypeStruct((b, s, RET_HEADS * RET_DK), BF16),
        compiler_params=_cparams("parallel", "parallel"),
        name="retention",
    )(proj3, proj3, proj3, proj3, cos2, sin2, decay, xi, zeta, gcb)


def _s5prep_kernel(lr_ref, li_ref, ldt_ref, br_ref, bi_ref, pr_ref, pi_ref, bbr_ref, bbi_ref):
    lr = jnp.minimum(lr_ref[...], -1e-4)
    li = li_ref[...]
    dt = jnp.exp(ldt_ref[...])
    mag = jnp.exp(lr * dt)
    ar = mag * jnp.cos(li * dt)
    ai = mag * jnp.sin(li * dt)
    nr, ni = ar - 1.0, ai
    den = lr * lr + li * li
    fr = (nr * lr + ni * li) / den
    fi = (ni * lr - nr * li) / den
    br = br_ref[...]
    bi = bi_ref[...]
    bbr_ref[...] = fr[None] * br - fi[None] * bi
    bbi_ref[...] = fr[None] * bi + fi[None] * br
    pr, pi = ar, ai
    pr_ref[0] = pr
    pi_ref[0] = pi
    for t in range(1, SUBLANES):
        pr, pi = pr * ar - pi * ai, pr * ai + pi * ar
        pr_ref[t] = pr
        pi_ref[t] = pi


def _s5_prep(lam_re, lam_im, log_dt, b_re, b_im):
    g, p = lam_re.shape
    pw = jax.ShapeDtypeStruct((SUBLANES, g, p), F32)
    bb = jax.ShapeDtypeStruct((S5_GROUP, g, p), F32)
    return pl.pallas_call(
        _s5prep_kernel,
        out_shape=[pw, pw, bb, bb],
        name="s5_prep",
    )(lam_re, lam_im, log_dt.reshape(g, 1), b_re.transpose(2, 0, 1), b_im.transpose(2, 0, 1))


SCAN_LANES = 256


def _s5_kernel(u_ref, bcat_ref, cre_ref, cim_ref, pr_ref, pi_ref, dskip_ref, wglu_ref, o_ref, bu_ref, carry_ref):
    n = S5_NSTATE
    t_len = u_ref.shape[0]

    @pl.when(pl.program_id(1) == 0)
    def _():
        carry_ref[...] = jnp.zeros_like(carry_ref)

    u = u_ref[...]
    bu_ref[...] = _dot(u.astype(BF16), bcat_ref[...])

    row = lax.broadcasted_iota(jnp.int32, (SUBLANES, SCAN_LANES), 0)
    for lo in range(0, n, SCAN_LANES):
        sl_r = slice(lo, lo + SCAN_LANES)
        sl_i = slice(n + lo, n + lo + SCAN_LANES)
        pw_r = pr_ref[:, sl_r]
        pw_i = pi_ref[:, sl_r]
        bcast = lambda t: jnp.broadcast_to(t, (SUBLANES, SCAN_LANES))
        levels = []
        for sh in (1, 2, 4):
            levels.append((sh,
                           jnp.where(row >= sh, bcast(pw_r[sh - 1:sh, :]), 0.0),
                           jnp.where(row >= sh, bcast(pw_i[sh - 1:sh, :]), 0.0)))
        a8r = bcast(pw_r[SUBLANES - 1:SUBLANES, :])
        a8i = bcast(pw_i[SUBLANES - 1:SUBLANES, :])

        def blk(r, carry, sl_r=sl_r, sl_i=sl_i, levels=levels, pw_r=pw_r, pw_i=pw_i, a8r=a8r, a8i=a8i, bcast=bcast):
            cr, ci = carry
            rs = pl.ds(pl.multiple_of(r * SUBLANES, SUBLANES), SUBLANES)
            xr = bu_ref[rs, sl_r]
            xi = bu_ref[rs, sl_i]
            for sh, mr, mi in levels:
                rr = pltpu.roll(xr, sh, 0)
                ri = pltpu.roll(xi, sh, 0)
                xr, xi = xr + mr * rr - mi * ri, xi + mr * ri + mi * rr
            bu_ref[rs, sl_r] = xr + pw_r * cr - pw_i * ci
            bu_ref[rs, sl_i] = xi + pw_r * ci + pw_i * cr
            lr_ = bcast(xr[SUBLANES - 1:SUBLANES, :])
            li_ = bcast(xi[SUBLANES - 1:SUBLANES, :])
            return (lr_ + a8r * cr - a8i * ci, li_ + a8r * ci + a8i * cr)

        cr, ci = lax.fori_loop(0, t_len // SUBLANES, blk, (carry_ref[:, sl_r], carry_ref[:, sl_i]))
        carry_ref[:, sl_r] = cr
        carry_ref[:, sl_i] = ci

    xr = bu_ref[:, :n].astype(BF16)
    xi = bu_ref[:, n:].astype(BF16)
    y = _dot(xr, cre_ref[...]) - _dot(xi, cim_ref[...]) + dskip_ref[...] * u
    gl = jax.nn.gelu(y)
    o_ref[...] = (gl * jax.nn.sigmoid(_dot(gl.astype(BF16), wglu_ref[...]))).astype(o_ref.dtype)


def _s5(proj3, lam_re, lam_im, log_dt, b_re, b_im, c_re, c_im, d_skip, w_glu):
    b, s, _ = proj3.shape
    g, p, hh = S5_GROUPS, S5_STATE, S5_GROUP
    n = S5_NSTATE
    pw_r, pw_i, bb_r, bb_i = _s5_prep(lam_re, lam_im, log_dt, b_re, b_im)
    eye = jnp.eye(g, dtype=F32)
    blk_in = lambda t: jnp.einsum('hgp,gk->ghkp', t, eye).reshape(g * hh, n)
    blk_out = lambda t: jnp.einsum('ghp,gk->gpkh', t, eye).reshape(n, g * hh)
    bcat = jnp.concatenate([blk_in(bb_r), blk_in(bb_i)], axis=1).astype(BF16)
    cre = blk_out(c_re).astype(BF16)
    cim = blk_out(c_im).astype(BF16)
    t_len = min(256, s)
    full = lambda shape: pl.BlockSpec(shape, lambda i, t: (0,) * len(shape))
    return pl.pallas_call(
        _s5_kernel,
        grid=(b, s // t_len),
        in_specs=[pl.BlockSpec((None, t_len, S5_WIDTH), lambda i, t: (i, t, COL_S5_U512)),
                  full((S5_WIDTH, 2 * n)), full((n, S5_WIDTH)), full((n, S5_WIDTH)),
                  full((SUBLANES, n)), full((SUBLANES, n)), full((1, S5_WIDTH)), full((S5_WIDTH, S5_WIDTH))],
        out_specs=pl.BlockSpec((None, t_len, S5_WIDTH), lambda i, t: (i, t, 0)),
        out_shape=jax.ShapeDtypeStruct((b, s, S5_WIDTH), BF16),
        scratch_shapes=[pltpu.VMEM((t_len, 2 * n), F32), pltpu.VMEM((SUBLANES, 2 * n), F32)],
        compiler_params=_cparams("parallel", "arbitrary"),
        name="s5",
    )(proj3, bcat, cre, cim, pw_r.reshape(SUBLANES, n), pw_i.reshape(SUBLANES, n),
      d_skip.reshape(1, S5_WIDTH), w_glu.astype(BF16))


def _bias_kernel(tbl_ref, o_ref):
    h = pl.program_id(0)
    r = lax.broadcasted_iota(jnp.int32, (Q_BLOCK, Q_BLOCK), 0)
    c = lax.broadcasted_iota(jnp.int32, (Q_BLOCK, Q_BLOCK), 1)
    max_exact = REL_BUCKETS // 2
    for d in range(2):
        n = jnp.maximum(r - c + Q_BLOCK * d, 0)
        large = max_exact + (jnp.log(jnp.maximum(n, 1).astype(F32) / max_exact)
                             / math.log(REL_MAX_DIST / max_exact) * (REL_BUCKETS - max_exact)).astype(jnp.int32)
        large = jnp.minimum(large, REL_BUCKETS - 1)
        bucket = jnp.where(n < max_exact, n, large)
        acc = jnp.zeros((Q_BLOCK, Q_BLOCK), F32)
        for bk in range(REL_BUCKETS):
            acc = jnp.where(bucket == bk, tbl_ref[bk, h], acc)
        o_ref[d] = acc


def _bias_blocks(rel_bias):
    return pl.pallas_call(
        _bias_kernel,
        grid=(DIFF_HEADS,),
        in_specs=[pl.BlockSpec(memory_space=pltpu.SMEM)],
        out_specs=pl.BlockSpec((None, 2, Q_BLOCK, Q_BLOCK), lambda h: (h, 0, 0, 0)),
        out_shape=jax.ShapeDtypeStruct((DIFF_HEADS, 2, Q_BLOCK, Q_BLOCK), F32),
        compiler_params=_cparams("arbitrary"),
        name="t5_bias_blocks",
    )(rel_bias)


def _diff_kernel(tbl_ref, q_ref, k_ref, v_ref, bias_ref, lq1_ref, lk1_ref, lq2_ref, lk2_ref, gs_ref, o_ref,
                 s_ref, m_ref, kb_ref, vb_ref, *, lam_init):
    qb = Q_BLOCK
    h = pl.program_id(1)
    n_q = q_ref.shape[0] // qb
    kb_ref[...] = k_ref[...].astype(BF16)
    vb_ref[...] = v_ref[...].astype(BF16)
    lam = (jnp.exp(jnp.sum(lq1_ref[...] * lk1_ref[...], axis=-1, keepdims=True))
           - jnp.exp(jnp.sum(lq2_ref[...] * lk2_ref[...], axis=-1, keepdims=True)) + lam_init)
    far_bias = tbl_ref[REL_BUCKETS - 1, h]
    lane = lax.broadcasted_iota(jnp.int32, (qb, qb), 1)
    rowi = lax.broadcasted_iota(jnp.int32, (qb, qb), 0)
    causal = lane <= rowi
    causal2 = jnp.concatenate([causal, causal], axis=0)
    b0 = bias_ref[0]
    b0 = jnp.concatenate([b0, b0], axis=0)
    b1 = bias_ref[1]
    b1 = jnp.concatenate([b1, b1], axis=0)
    gs = gs_ref[...] * (1.0 - lam_init)

    def kblk(j):
        return pl.ds(pl.multiple_of(j * qb, qb), qb)

    def qblock(i, _):
        q = q_ref[kblk(i), :] * (DIFF_DQK ** -0.5)
        q2 = jnp.concatenate([jnp.where(lane < DIFF_DQK, q, 0.0), jnp.where(lane >= DIFF_DQK, q, 0.0)],
                             axis=0).astype(BF16)

        def score(j):
            return lax.dot_general(q2, kb_ref[kblk(j), :], NT, preferred_element_type=F32)

        m_ref[...] = jnp.full(m_ref.shape, -jnp.inf, F32)

        def far(j, _):
            s = score(j) + far_bias
            s_ref[j] = s
            m_ref[...] = jnp.maximum(m_ref[...], s)
            return 0

        lax.fori_loop(0, jnp.maximum(i - 1, 0), far, 0)

        @pl.when(i >= 1)
        def _():
            s = score(i - 1) + b1
            s_ref[i - 1] = s
            m_ref[...] = jnp.maximum(m_ref[...], s)

        s = jnp.where(causal2, score(i) + b0, -jnp.inf)
        s_ref[i] = s
        m = jnp.max(jnp.maximum(m_ref[...], s), axis=-1, keepdims=True)

        def expo(j, l_acc):
            e = jnp.exp(s_ref[j] - m)
            s_ref[j] = e
            return l_acc + e

        l_acc = lax.fori_loop(0, i + 1, expo, jnp.zeros((2 * qb, qb), F32))
        l_sum = jnp.sum(l_acc, axis=-1, keepdims=True)
        w1 = 1.0 / l_sum[:qb]
        w2 = lam / l_sum[qb:]

        def pv(j, acc):
            e = s_ref[j]
            p = e[:qb] * w1 - e[qb:] * w2
            return acc + _dot(p.astype(BF16), vb_ref[kblk(j), :])

        o = lax.fori_loop(0, i + 1, pv, jnp.zeros((qb, DIFF_DV), F32))
        o = o * lax.rsqrt(jnp.mean(o * o, axis=-1, keepdims=True) + EPS) * gs
        o_ref[kblk(i), :] = o.astype(o_ref.dtype)
        return 0

    lax.fori_loop(0, n_q, qblock, 0)


def _diff_attn(proj3, rel_bias, bias_blocks, lq1, lk1, lq2, lk2, g_subln, lam_init):
    b, s, _ = proj3.shape
    tok_blk = lambda col: pl.BlockSpec((None, s, LANES), lambda i, h: (i, 0, col + h))
    vec = lambda n: pl.BlockSpec((1, n), lambda i, h: (0, 0))
    return pl.pallas_call(
        functools.partial(_diff_kernel, lam_init=lam_init),
        grid=(b, DIFF_HEADS),
        in_specs=[pl.BlockSpec(memory_space=pltpu.SMEM),
                  tok_blk(COL_DIFF_Q), tok_blk(COL_DIFF_K), tok_blk(COL_DIFF_V),
                  pl.BlockSpec((None, 2, Q_BLOCK, Q_BLOCK), lambda i, h: (h, 0, 0, 0)),
                  vec(DIFF_DQK), vec(DIFF_DQK), vec(DIFF_DQK), vec(DIFF_DQK), vec(DIFF_DV)],
        out_specs=pl.BlockSpec((None, s, LANES), lambda i, h: (i, 0, h)),
        out_shape=jax.ShapeDtypeStruct((b, s, DIFF_HEADS * DIFF_DV), BF16),
        scratch_shapes=[pltpu.VMEM((s // Q_BLOCK, 2 * Q_BLOCK, Q_BLOCK), F32),
                        pltpu.VMEM((2 * Q_BLOCK, Q_BLOCK), F32),
                        pltpu.VMEM((s, LANES), BF16), pltpu.VMEM((s, LANES), BF16)],
        compiler_params=_cparams("parallel", "parallel"),
        name="diff_attn",
    )(rel_bias, proj3, proj3, proj3, bias_blocks,
      lq1.reshape(1, -1), lk1.reshape(1, -1), lq2.reshape(1, -1), lk2.reshape(1, -1), g_subln.reshape(1, -1))


def _outproj_kernel(yr_ref, ys_ref, yd_ref, x_ref, w_ref, g_ref, o_ref):
    n_r = yr_ref.shape[1]
    n_s = ys_ref.shape[1]
    acc = _dot(yr_ref[...], w_ref[:n_r, :])
    acc = acc + _dot(ys_ref[...], w_ref[n_r:n_r + n_s, :])
    acc = acc + _dot(yd_ref[...], w_ref[n_r + n_s:, :])
    o_ref[...] = x_ref[...] + _rms(acc, g_ref[...])


def _outproj(y_ret, y_s5, y_diff, x2, w, g):
    tok, d = x2.shape
    tm = min(512, tok)
    row = lambda n: pl.BlockSpec((tm, n), lambda i: (i, 0))
    return pl.pallas_call(
        _outproj_kernel,
        grid=(tok // tm,),
        in_specs=[row(y_ret.shape[1]), row(y_s5.shape[1]), row(y_diff.shape[1]), row(d),
                  pl.BlockSpec((d, d), lambda i: (0, 0)), pl.BlockSpec((1, d), lambda i: (0, 0))],
        out_specs=row(d),
        out_shape=jax.ShapeDtypeStruct((tok, d), F32),
        compiler_params=_cparams("parallel"),
        name="outproj",
    )(y_ret, y_s5, y_diff, x2, w, g)


def _memkv_kernel(m_ref, g_ref, w_ref, o_ref):
    o_ref[...] = _dot(_rms(m_ref[...], g_ref[...]).astype(BF16), w_ref[...]).astype(o_ref.dtype)


def _memkv(mem, g, w):
    b, m, d = mem.shape
    n = w.shape[1]
    return pl.pallas_call(
        _memkv_kernel,
        grid=(b,),
        in_specs=[pl.BlockSpec((None, m, d), lambda i: (i, 0, 0)),
                  pl.BlockSpec((1, d), lambda i: (0, 0)), pl.BlockSpec((d, n), lambda i: (0, 0))],
        out_specs=pl.BlockSpec((None, m, n), lambda i: (i, 0, 0)),
        out_shape=jax.ShapeDtypeStruct((b, m, n), BF16),
        compiler_params=_cparams("parallel"),
        name="mem_kv",
    )(mem, g, w)


def _xattn_kernel(x_ref, kv_ref, wq_ref, wo_ref, gpre_ref, gpost_ref, o_ref):
    x = x_ref[...]
    hn = _rms(x, gpre_ref[...]).astype(BF16)
    q = (_dot(hn, wq_ref[...]) * (X_HEAD_DIM ** -0.5)).astype(BF16)
    outs = []
    for hd in range(X_HEADS):
        lo = hd * X_HEAD_DIM
        kh = kv_ref[:, lo:lo + X_HEAD_DIM]
        vh = kv_ref[:, X_WIDTH + lo:X_WIDTH + lo + X_HEAD_DIM]
        s = lax.dot_general(q[:, lo:lo + X_HEAD_DIM], kh, NT, preferred_element_type=F32)
        e = jnp.exp(s - jnp.max(s, axis=-1, keepdims=True))
        l_sum = jnp.sum(e, axis=-1, keepdims=True)
        outs.append(_dot(e.astype(BF16), vh) / l_sum)
    o = jnp.concatenate(outs, axis=-1).astype(BF16)
    o_ref[...] = x + _rms(_dot(o, wo_ref[...]), gpost_ref[...])


def _xattn(x3, kv, wq, wo, gpre, gpost):
    b, s, d = x3.shape
    m = kv.shape[1]
    tq = min(512, s)
    full = lambda shape: pl.BlockSpec(shape, lambda i, t: (0,) * len(shape))
    return pl.pallas_call(
        _xattn_kernel,
        grid=(b, s // tq),
        in_specs=[pl.BlockSpec((None, tq, d), lambda i, t: (i, t, 0)),
                  pl.BlockSpec((None, m, 2 * X_WIDTH), lambda i, t: (i, 0, 0)),
                  full((d, X_WIDTH)), full((X_WIDTH, d)), full((1, d)), full((1, d))],
        out_specs=pl.BlockSpec((None, tq, d), lambda i, t: (i, t, 0)),
        out_shape=jax.ShapeDtypeStruct((b, s, d), F32),
        compiler_params=_cparams("parallel", "parallel"),
        name="xattn",
    )(x3, kv, wq, wo, gpre, gpost)


def _mlp_kernel(x_ref, wu_ref, wd_ref, gpre_ref, gpost_ref, o_ref, h_ref, acc_ref):
    j = pl.program_id(1)

    @pl.when(j == 0)
    def _():
        h_ref[...] = _rms(x_ref[...], gpre_ref[...]).astype(BF16)
        acc_ref[...] = jnp.zeros_like(acc_ref)

    u = jnp.maximum(_dot(h_ref[...], wu_ref[...]), 0.0)
    acc_ref[...] += _dot((u * u).astype(BF16), wd_ref[...])

    @pl.when(j == pl.num_programs(1) - 1)
    def _():
        o_ref[...] = x_ref[...] + _rms(acc_ref[...], gpost_ref[...])


def _mlp(x2, wu, wd, gpre, gpost):
    tok, d = x2.shape
    ff = wu.shape[1]
    tm = min(512, tok)
    tf = 512
    return pl.pallas_call(
        _mlp_kernel,
        grid=(tok // tm, ff // tf),
        in_specs=[pl.BlockSpec((tm, d), lambda i, j: (i, 0)),
                  pl.BlockSpec((d, tf), lambda i, j: (0, j)),
                  pl.BlockSpec((tf, d), lambda i, j: (j, 0)),
                  pl.BlockSpec((1, d), lambda i, j: (0, 0)), pl.BlockSpec((1, d), lambda i, j: (0, 0))],
        out_specs=pl.BlockSpec((tm, d), lambda i, j: (i, 0)),
        out_shape=jax.ShapeDtypeStruct((tok, d), F32),
        scratch_shapes=[pltpu.VMEM((tm, d), BF16), pltpu.VMEM((tm, d), F32)],
        compiler_params=_cparams("parallel", "arbitrary"),
        name="mlp",
    )(x2, wu, wd, gpre, gpost)


def kernel(x, mem, positions, rel_bias, w_in, w_out, lam_re, lam_im, log_dt, b_re, b_im, c_re, c_im, d_skip, w_glu, lam_q1, lam_k1, lam_q2, lam_k2, g_subln, w_xq, w_xkv, w_xo, w_up, w_down, g_mix_pre, g_mix_post, g_mem, g_x_pre, g_x_post, g_mlp_pre, g_mlp_post):
    b, s, d = x.shape
    tok = b * s
    row = lambda g: g.reshape(1, -1)
    cos2, sin2 = _rot_tables(positions)
    bias_blocks = _bias_blocks(rel_bias)
    x2 = x.reshape(tok, d)
    for l in range(DEPTH):
        lam_init = 0.8 - 0.6 * math.exp(-0.3 * l)
        proj3 = _inproj(x2, row(g_mix_pre[l]), w_in[l].astype(BF16)).reshape(b, s, IN_WIDTH)
        y_ret = _retention(proj3, cos2, sin2)
        y_s5 = _s5(proj3, lam_re[l], lam_im[l], log_dt[l], b_re[l], b_im[l], c_re[l], c_im[l], d_skip[l], w_glu[l])
        y_diff = _diff_attn(proj3, rel_bias, bias_blocks, lam_q1[l], lam_k1[l], lam_q2[l], lam_k2[l],
                            g_subln[l], lam_init)
        x2 = _outproj(y_ret.reshape(tok, -1), y_s5.reshape(tok, -1), y_diff.reshape(tok, -1), x2,
                      w_out[l].astype(BF16), row(g_mix_post[l]))
        kv = _memkv(mem, row(g_mem[l]), w_xkv[l].astype(BF16))
        x2 = _xattn(x2.reshape(b, s, d), kv, w_xq[l].astype(BF16), w_xo[l].astype(BF16),
                    row(g_x_pre[l]), row(g_x_post[l])).reshape(tok, d)
        x2 = _mlp(x2, w_up[l].astype(BF16), w_down[l].astype(BF16), row(g_mlp_pre[l]), row(g_mlp_post[l]))
    return x2.reshape(b, s, d)
```

```python
import functools
import math

import jax
import jax.numpy as jnp
from jax import lax
from jax.experimental import pallas as pl
from jax.experimental.pallas import tpu as pltpu

F32 = jnp.float32
BF16 = jnp.bfloat16

D_MODEL = 2048
DEPTH = 2
MEM_LEN = 256
RET_HEADS = 4
RET_DK = 128
RET_CHUNK = 128
RET_WIDTH = RET_HEADS * RET_DK
S5_WIDTH = 512
S5_GROUP = 16
S5_GROUPS = 32
S5_STATE = 64
S5_NSTATE = S5_GROUPS * S5_STATE
DIFF_HEADS = 8
DIFF_DV = 128
DIFF_DQK = 64
REL_BUCKETS = 32
REL_MAX_DIST = 128
X_HEADS = 4
X_HEAD_DIM = 128
X_WIDTH = X_HEADS * X_HEAD_DIM
D_FF = 4 * D_MODEL
EPS = 1e-6
IN_WIDTH = 5632
COL_RET_Q, COL_RET_K, COL_RET_V, COL_RET_G, COL_S5_U = 0, 1, 2, 3, 4
COL_DIFF_Q, COL_DIFF_K, COL_DIFF_V = 20, 28, 36

LANES = 128
SUBLANES = 8
VMEM_LIMIT = 56 * 1024 * 1024

NT = (((1,), (1,)), ((), ()))
TN = (((0,), (0,)), ((), ()))


def _cparams(*sem):
    return pltpu.CompilerParams(dimension_semantics=sem, vmem_limit_bytes=VMEM_LIMIT)


def _rms(x, g):
    ms = jnp.mean(x * x, axis=-1, keepdims=True)
    return x * lax.rsqrt(ms + EPS) * g


def _dot(a, b):
    return jnp.dot(a, b, preferred_element_type=F32)


def _inproj_kernel(x_ref, g_ref, w_ref, o_ref, h_ref):
    @pl.when(pl.program_id(1) == 0)
    def _():
        h_ref[...] = _rms(x_ref[...], g_ref[...]).astype(BF16)

    o_ref[...] = _dot(h_ref[...], w_ref[...]).astype(o_ref.dtype)


def _inproj(x2, g, w):
    tok, d = x2.shape
    n = w.shape[1]
    tm = min(1024, tok)
    tn = 512
    return pl.pallas_call(
        _inproj_kernel,
        grid=(tok // tm, n // tn),
        in_specs=[
            pl.BlockSpec((tm, d), lambda i, j: (i, 0)),
            pl.BlockSpec((1, d), lambda i, j: (0, 0)),
            pl.BlockSpec((d, tn), lambda i, j: (0, j)),
        ],
        out_specs=pl.BlockSpec((tm, tn), lambda i, j: (i, j)),
        out_shape=jax.ShapeDtypeStruct((tok, n), BF16),
        scratch_shapes=[pltpu.VMEM((tm, d), BF16)],
        compiler_params=_cparams("parallel", "arbitrary"),
        name="inproj",
    )(x2, g, w)


def _rot_kernel(pos_ref, cos_ref, sin_ref):
    pos = pos_ref[...].astype(F32)
    lane = lax.broadcasted_iota(jnp.int32, (1, LANES), 1)
    half = RET_DK // 2
    j = jnp.where(lane < half, lane, lane - half).astype(F32)
    inv = jnp.exp(j * (-math.log(10000.0) / half))
    ang = pos * inv
    cos_ref[...] = jnp.cos(ang)
    s = jnp.sin(ang)
    sin_ref[...] = jnp.where(lane < half, -s, s)


def _rot_tables(positions):
    b, s = positions.shape
    out = jax.ShapeDtypeStruct((b, s, LANES), F32)
    return pl.pallas_call(
        _rot_kernel,
        grid=(b,),
        in_specs=[pl.BlockSpec((None, s, 1), lambda i: (i, 0, 0))],
        out_specs=[pl.BlockSpec((None, s, LANES), lambda i: (i, 0, 0))] * 2,
        out_shape=[out, out],
        compiler_params=_cparams("parallel"),
        name="rot_tables",
    )(positions.reshape(b, s, 1))


RET_TBLK = 512


def _ret_kernel(q_ref, k_ref, v_ref, g_ref, cos_ref, sin_ref, dec_ref, xi_ref, zeta_ref, gc_ref, o_ref, r_ref):
    c_len = RET_CHUNK
    scale = RET_DK ** -0.5

    @pl.when(pl.program_id(1) == 0)
    def _():
        r_ref[...] = jnp.zeros_like(r_ref)

    for c in range(q_ref.shape[0] // c_len):
        rows = slice(c * c_len, (c + 1) * c_len)
        cs = cos_ref[rows, :]
        sn = sin_ref[rows, :]
        for h in range(RET_HEADS):
            cols = slice(h * RET_DK, (h + 1) * RET_DK)
            q = q_ref[rows, cols].astype(F32)
            k = k_ref[rows, cols].astype(F32)
            qr = q * cs + pltpu.roll(q, RET_DK // 2, 1) * sn
            kr = (k * cs + pltpu.roll(k, RET_DK // 2, 1) * sn) * scale
            vb = v_ref[rows, cols]
            r_state = r_ref[h]
            inner = lax.dot_general(qr.astype(BF16), kr.astype(BF16), NT, preferred_element_type=F32) * dec_ref[h]
            lhs = jnp.concatenate([inner.astype(BF16), (qr * xi_ref[h]).astype(BF16)], axis=1)
            rhs = jnp.concatenate([vb, r_state.astype(BF16)], axis=0)
            o = _dot(lhs, rhs)
            r_ref[h] = gc_ref[h] * r_state + lax.dot_general((kr * zeta_ref[h]).astype(BF16), vb, TN,
                                                             preferred_element_type=F32)
            o = o * lax.rsqrt(jnp.mean(o * o, axis=-1, keepdims=True) + EPS)
            gate = g_ref[rows, cols].astype(F32)
            o_ref[rows, cols] = (o * (gate * jax.nn.sigmoid(gate))).astype(o_ref.dtype)


def _ret_consts():
    h, c = RET_HEADS, RET_CHUNK
    log_g = jnp.log(1.0 - 2.0 ** (-5.0 - jnp.arange(h, dtype=F32)))
    idx = jnp.arange(c, dtype=F32)
    dist = idx[:, None] - idx[None, :]
    decay = jnp.where(dist >= 0, jnp.exp(jnp.maximum(dist, 0.0)[None] * log_g[:, None, None]), 0.0)
    xi = jnp.exp((idx + 1.0)[None, :] * log_g[:, None])
    zeta = jnp.exp((c - 1.0 - idx)[None, :] * log_g[:, None])
    g_chunk = jnp.exp(c * log_g)
    bc = lambda t: jnp.broadcast_to(t[:, :, None], (h, c, LANES))
    gcb = jnp.broadcast_to(g_chunk[:, None, None], (h, c, LANES))
    return decay, bc(xi), bc(zeta), gcb


def _retention(proj3, cos2, sin2):
    b, s, _ = proj3.shape
    tb = min(RET_TBLK, s)
    decay, xi, zeta, gcb = _ret_consts()
    tok_blk = lambda col: pl.BlockSpec((None, tb, RET_WIDTH), lambda i, t: (i, t, col))
    pos_blk = pl.BlockSpec((None, tb, LANES), lambda i, t: (i, t, 0))
    const = pl.BlockSpec((RET_HEADS, RET_CHUNK, LANES), lambda i, t: (0, 0, 0))
    return pl.pallas_call(
        _ret_kernel,
        grid=(b, s // tb),
        in_specs=[tok_blk(COL_RET_Q), tok_blk(COL_RET_K), tok_blk(COL_RET_V), tok_blk(COL_RET_G),
                  pos_blk, pos_blk, const, const, const, const],
        out_specs=pl.BlockSpec((None, tb, RET_WIDTH), lambda i, t: (i, t, 0)),
        out_shape=jax.ShapeDtypeStruct((b, s, RET_WIDTH), BF16),
        scratch_shapes=[pltpu.VMEM((RET_HEADS, RET_DK, RET_DK), F32)],
        compiler_params=_cparams("parallel", "arbitrary"),
        name="retention",
    )(proj3, proj3, proj3, proj3, cos2, sin2, decay, xi, zeta, gcb)


def _s5prep_kernel(lr_ref, li_ref, ldt_ref, br_ref, bi_ref, ar_ref, ai_ref, bbr_ref, bbi_ref):
    lr = jnp.minimum(lr_ref[...], -1e-4)
    li = li_ref[...]
    dt = jnp.exp(ldt_ref[...])
    mag = jnp.exp(lr * dt)
    ar = mag * jnp.cos(li * dt)
    ai = mag * jnp.sin(li * dt)
    nr, ni = ar - 1.0, ai
    den = lr * lr + li * li
    fr = (nr * lr + ni * li) / den
    fi = (ni * lr - nr * li) / den
    br = br_ref[...]
    bi = bi_ref[...]
    bbr_ref[...] = fr[None] * br - fi[None] * bi
    bbi_ref[...] = fr[None] * bi + fi[None] * br
    ar_ref[...] = ar
    ai_ref[...] = ai


def _s5_prep(lam_re, lam_im, log_dt, b_re, b_im):
    g, p = lam_re.shape
    ab = jax.ShapeDtypeStruct((g, p), F32)
    bb = jax.ShapeDtypeStruct((S5_GROUP, g, p), F32)
    return pl.pallas_call(
        _s5prep_kernel,
        out_shape=[ab, ab, bb, bb],
        name="s5_prep",
    )(lam_re, lam_im, log_dt.reshape(g, 1), b_re.transpose(2, 0, 1), b_im.transpose(2, 0, 1))


S5_SCAN_LANES = 512
S5_TSTEPS = 32


def _s5_kernel(u_ref, bcat_ref, cre_ref, cim_ref, ar_ref, ai_ref, dskip_ref, wglu_ref, o_ref, bu_ref, st_ref, *, nb):
    n = S5_NSTATE
    t_steps = u_ref.shape[0] // nb

    @pl.when(pl.program_id(0) == 0)
    def _():
        st_ref[...] = jnp.zeros_like(st_ref)

    u = u_ref[...]
    bu_ref[...] = _dot(u, bcat_ref[...])

    for lo in range(0, n, S5_SCAN_LANES):
        sl_r = slice(lo, lo + S5_SCAN_LANES)
        sl_i = slice(n + lo, n + lo + S5_SCAN_LANES)
        a_r = jnp.broadcast_to(ar_ref[:, sl_r], (nb, S5_SCAN_LANES))
        a_i = jnp.broadcast_to(ai_ref[:, sl_r], (nb, S5_SCAN_LANES))

        def step(t, carry, sl_r=sl_r, sl_i=sl_i, a_r=a_r, a_i=a_i):
            xr, xi = carry
            rs = pl.ds(pl.multiple_of(t * nb, nb), nb)
            nr = a_r * xr - a_i * xi + bu_ref[rs, sl_r]
            ni = a_r * xi + a_i * xr + bu_ref[rs, sl_i]
            bu_ref[rs, sl_r] = nr
            bu_ref[rs, sl_i] = ni
            return nr, ni

        xr, xi = lax.fori_loop(0, t_steps, step, (st_ref[:, sl_r], st_ref[:, sl_i]), unroll=2)
        st_ref[:, sl_r] = xr
        st_ref[:, sl_i] = xi

    xr = bu_ref[:, :n].astype(BF16)
    xi = bu_ref[:, n:].astype(BF16)
    y = _dot(xr, cre_ref[...]) - _dot(xi, cim_ref[...]) + dskip_ref[...] * u.astype(F32)
    gl = jax.nn.gelu(y)
    o_ref[...] = (gl * jax.nn.sigmoid(_dot(gl.astype(BF16), wglu_ref[...]))).astype(o_ref.dtype)


def _s5(proj3, lam_re, lam_im, log_dt, b_re, b_im, c_re, c_im, d_skip, w_glu):
    b, s, _ = proj3.shape
    g, hh = S5_GROUPS, S5_GROUP
    n = S5_NSTATE
    a_r, a_i, bb_r, bb_i = _s5_prep(lam_re, lam_im, log_dt, b_re, b_im)
    eye = jnp.eye(g, dtype=F32)
    blk_in = lambda t: jnp.einsum('hgp,gk->ghkp', t, eye).reshape(g * hh, n)
    blk_out = lambda t: jnp.einsum('ghp,gk->gpkh', t, eye).reshape(n, g * hh)
    bcat = jnp.concatenate([blk_in(bb_r), blk_in(bb_i)], axis=1).astype(BF16)
    cre = blk_out(c_re).astype(BF16)
    cim = blk_out(c_im).astype(BF16)
    u_tb = proj3[:, :, COL_S5_U * S5_WIDTH:(COL_S5_U + 1) * S5_WIDTH].transpose(1, 0, 2).reshape(s * b, S5_WIDTH)
    tm = b * min(S5_TSTEPS, s)
    full = lambda shape: pl.BlockSpec(shape, lambda t: (0,) * len(shape))
    y_tb = pl.pallas_call(
        functools.partial(_s5_kernel, nb=b),
        grid=(s * b // tm,),
        in_specs=[pl.BlockSpec((tm, S5_WIDTH), lambda t: (t, 0)),
                  full((S5_WIDTH, 2 * n)), full((n, S5_WIDTH)), full((n, S5_WIDTH)),
                  full((1, n)), full((1, n)), full((1, S5_WIDTH)), full((S5_WIDTH, S5_WIDTH))],
        out_specs=pl.BlockSpec((tm, S5_WIDTH), lambda t: (t, 0)),
        out_shape=jax.ShapeDtypeStruct((s * b, S5_WIDTH), BF16),
        scratch_shapes=[pltpu.VMEM((tm, 2 * n), F32), pltpu.VMEM((b, 2 * n), F32)],
        compiler_params=_cparams("arbitrary"),
        name="s5",
    )(u_tb, bcat, cre, cim, a_r.reshape(1, n), a_i.reshape(1, n), d_skip.reshape(1, S5_WIDTH), w_glu.astype(BF16))
    return y_tb.reshape(s, b, S5_WIDTH).transpose(1, 0, 2)


ATT_TILE = 256


def _bias_kernel(tbl_ref, o_ref):
    h = pl.program_id(0)
    t = ATT_TILE
    r = lax.broadcasted_iota(jnp.int32, (t, t), 0)
    c = lax.broadcasted_iota(jnp.int32, (t, t), 1)
    max_exact = REL_BUCKETS // 2
    for d in range(2):
        n = jnp.maximum(r - c + t * d, 0)
        large = max_exact + (jnp.log(jnp.maximum(n, 1).astype(F32) / max_exact)
                             / math.log(REL_MAX_DIST / max_exact) * (REL_BUCKETS - max_exact)).astype(jnp.int32)
        large = jnp.minimum(large, REL_BUCKETS - 1)
        bucket = jnp.where(n < max_exact, n, large)
        acc = jnp.zeros((t, t), F32)
        for bk in range(REL_BUCKETS):
            acc = jnp.where(bucket == bk, tbl_ref[bk, h], acc)
        o_ref[d] = acc


def _bias_blocks(rel_bias):
    t = ATT_TILE
    return pl.pallas_call(
        _bias_kernel,
        grid=(DIFF_HEADS,),
        in_specs=[pl.BlockSpec(memory_space=pltpu.SMEM)],
        out_specs=pl.BlockSpec((None, 2, t, t), lambda h: (h, 0, 0, 0)),
        out_shape=jax.ShapeDtypeStruct((DIFF_HEADS, 2, t, t), F32),
        compiler_params=_cparams("arbitrary"),
        name="t5_bias_blocks",
    )(rel_bias)


def _diff_kernel(tbl_ref, q_ref, k_ref, v_ref, bias_ref, lq1_ref, lk1_ref, lq2_ref, lk2_ref, gs_ref, o_ref,
                 s_ref, *, lam_init):
    t = ATT_TILE
    h = pl.program_id(1)
    n_tiles = q_ref.shape[0] // t
    lam = (jnp.exp(jnp.sum(lq1_ref[...] * lk1_ref[...], axis=-1, keepdims=True))
           - jnp.exp(jnp.sum(lq2_ref[...] * lk2_ref[...], axis=-1, keepdims=True)) + lam_init)
    far_bias = tbl_ref[REL_BUCKETS - 1, h]
    lane_q = lax.broadcasted_iota(jnp.int32, (t, LANES), 1)
    col = lax.broadcasted_iota(jnp.int32, (2 * t, t), 1)
    row = lax.broadcasted_iota(jnp.int32, (2 * t, t), 0)
    causal2 = col <= jnp.where(row < t, row, row - t)
    gs = gs_ref[...] * (1.0 - lam_init)

    for i in range(n_tiles):
        q = q_ref[i * t:(i + 1) * t, :].astype(F32) * (DIFF_DQK ** -0.5)
        q2 = jnp.concatenate([jnp.where(lane_q < DIFF_DQK, q, 0.0), jnp.where(lane_q >= DIFF_DQK, q, 0.0)],
                             axis=0).astype(BF16)
        m_acc = None
        for j in range(i + 1):
            s = lax.dot_general(q2, k_ref[j * t:(j + 1) * t, :], NT, preferred_element_type=F32)
            if j == i:
                b0 = bias_ref[0]
                s = jnp.where(causal2, s + jnp.concatenate([b0, b0], axis=0), -jnp.inf)
            elif j == i - 1:
                b1 = bias_ref[1]
                s = s + jnp.concatenate([b1, b1], axis=0)
            else:
                s = s + far_bias
            s_ref[j] = s
            mj = jnp.maximum(s[:, :LANES], s[:, LANES:])
            m_acc = mj if m_acc is None else jnp.maximum(m_acc, mj)
        m = jnp.max(m_acc, axis=-1, keepdims=True)
        l_acc = jnp.zeros((2 * t, LANES), F32)
        for j in range(i + 1):
            e = jnp.exp(s_ref[j] - m)
            s_ref[j] = e
            l_acc = l_acc + (e[:, :LANES] + e[:, LANES:])
        l_sum = jnp.sum(l_acc, axis=-1, keepdims=True)
        w1 = 1.0 / l_sum[:t]
        w2 = lam / l_sum[t:]
        o = jnp.zeros((t, DIFF_DV), F32)
        for j in range(i + 1):
            e = s_ref[j]
            p = e[:t] * w1 - e[t:] * w2
            o = o + _dot(p.astype(BF16), v_ref[j * t:(j + 1) * t, :])
        o = o * lax.rsqrt(jnp.mean(o * o, axis=-1, keepdims=True) + EPS) * gs
        o_ref[i * t:(i + 1) * t, :] = o.astype(o_ref.dtype)


def _diff_attn(proj3, rel_bias, bias_blocks, lq1, lk1, lq2, lk2, g_subln, lam_init):
    b, s, _ = proj3.shape
    t = ATT_TILE
    tok_blk = lambda col: pl.BlockSpec((None, s, LANES), lambda i, h: (i, 0, col + h))
    vec = lambda n: pl.BlockSpec((1, n), lambda i, h: (0, 0))
    return pl.pallas_call(
        functools.partial(_diff_kernel, lam_init=lam_init),
        grid=(b, DIFF_HEADS),
        in_specs=[pl.BlockSpec(memory_space=pltpu.SMEM),
                  tok_blk(COL_DIFF_Q), tok_blk(COL_DIFF_K), tok_blk(COL_DIFF_V),
                  pl.BlockSpec((None, 2, t, t), lambda i, h: (h, 0, 0, 0)),
                  vec(DIFF_DQK), vec(DIFF_DQK), vec(DIFF_DQK), vec(DIFF_DQK), vec(DIFF_DV)],
        out_specs=pl.BlockSpec((None, s, LANES), lambda i, h: (i, 0, h)),
        out_shape=jax.ShapeDtypeStruct((b, s, DIFF_HEADS * DIFF_DV), BF16),
        scratch_shapes=[pltpu.VMEM((s // t, 2 * t, t), F32)],
        compiler_params=_cparams("parallel", "parallel"),
        name="diff_attn",
    )(rel_bias, proj3, proj3, proj3, bias_blocks,
      lq1.reshape(1, -1), lk1.reshape(1, -1), lq2.reshape(1, -1), lk2.reshape(1, -1), g_subln.reshape(1, -1))


def _outproj_kernel(yr_ref, ys_ref, yd_ref, x_ref, w_ref, g_ref, o_ref):
    n_r = yr_ref.shape[1]
    n_s = ys_ref.shape[1]
    acc = _dot(yr_ref[...], w_ref[:n_r, :])
    acc = acc + _dot(ys_ref[...], w_ref[n_r:n_r + n_s, :])
    acc = acc + _dot(yd_ref[...], w_ref[n_r + n_s:, :])
    o_ref[...] = x_ref[...] + _rms(acc, g_ref[...])


def _outproj(y_ret, y_s5, y_diff, x2, w, g):
    tok, d = x2.shape
    tm = min(512, tok)
    row = lambda n: pl.BlockSpec((tm, n), lambda i: (i, 0))
    return pl.pallas_call(
        _outproj_kernel,
        grid=(tok // tm,),
        in_specs=[row(y_ret.shape[1]), row(y_s5.shape[1]), row(y_diff.shape[1]), row(d),
                  pl.BlockSpec((d, d), lambda i: (0, 0)), pl.BlockSpec((1, d), lambda i: (0, 0))],
        out_specs=row(d),
        out_shape=jax.ShapeDtypeStruct((tok, d), F32),
        compiler_params=_cparams("parallel"),
        name="outproj",
    )(y_ret, y_s5, y_diff, x2, w, g)


def _memkv_kernel(m_ref, g_ref, w_ref, o_ref):
    o_ref[...] = _dot(_rms(m_ref[...], g_ref[...]).astype(BF16), w_ref[...]).astype(o_ref.dtype)


def _memkv(mem, g, w):
    b, m, d = mem.shape
    n = w.shape[1]
    return pl.pallas_call(
        _memkv_kernel,
        grid=(b,),
        in_specs=[pl.BlockSpec((None, m, d), lambda i: (i, 0, 0)),
                  pl.BlockSpec((1, d), lambda i: (0, 0)), pl.BlockSpec((d, n), lambda i: (0, 0))],
        out_specs=pl.BlockSpec((None, m, n), lambda i: (i, 0, 0)),
        out_shape=jax.ShapeDtypeStruct((b, m, n), BF16),
        compiler_params=_cparams("parallel"),
        name="mem_kv",
    )(mem, g, w)


def _xattn_kernel(x_ref, kv_ref, wq_ref, wo_ref, gpre_ref, gpost_ref, o_ref):
    x = x_ref[...]
    hn = _rms(x, gpre_ref[...]).astype(BF16)
    q = (_dot(hn, wq_ref[...]) * (X_HEAD_DIM ** -0.5)).astype(BF16)
    outs = []
    for hd in range(X_HEADS):
        lo = hd * X_HEAD_DIM
        kh = kv_ref[:, lo:lo + X_HEAD_DIM]
        vh = kv_ref[:, X_WIDTH + lo:X_WIDTH + lo + X_HEAD_DIM]
        s = lax.dot_general(q[:, lo:lo + X_HEAD_DIM], kh, NT, preferred_element_type=F32)
        e = jnp.exp(s - jnp.max(s, axis=-1, keepdims=True))
        l_sum = jnp.sum(e, axis=-1, keepdims=True)
        outs.append(_dot(e.astype(BF16), vh) / l_sum)
    o = jnp.concatenate(outs, axis=-1).astype(BF16)
    o_ref[...] = x + _rms(_dot(o, wo_ref[...]), gpost_ref[...])


def _xattn(x3, kv, wq, wo, gpre, gpost):
    b, s, d = x3.shape
    m = kv.shape[1]
    tq = min(512, s)
    full = lambda shape: pl.BlockSpec(shape, lambda i, t: (0,) * len(shape))
    return pl.pallas_call(
        _xattn_kernel,
        grid=(b, s // tq),
        in_specs=[pl.BlockSpec((None, tq, d), lambda i, t: (i, t, 0)),
                  pl.BlockSpec((None, m, 2 * X_WIDTH), lambda i, t: (i, 0, 0)),
                  full((d, X_WIDTH)), full((X_WIDTH, d)), full((1, d)), full((1, d))],
        out_specs=pl.BlockSpec((None, tq, d), lambda i, t: (i, t, 0)),
        out_shape=jax.ShapeDtypeStruct((b, s, d), F32),
        compiler_params=_cparams("parallel", "parallel"),
        name="xattn",
    )(x3, kv, wq, wo, gpre, gpost)


def _mlp_kernel(x_ref, wu_ref, wd_ref, gpre_ref, gpost_ref, o_ref, h_ref, acc_ref):
    j = pl.program_id(1)

    @pl.when(j == 0)
    def _():
        h_ref[...] = _rms(x_ref[...], gpre_ref[...]).astype(BF16)
        acc_ref[...] = jnp.zeros_like(acc_ref)

    u = jnp.maximum(_dot(h_ref[...], wu_ref[...]), 0.0)
    acc_ref[...] += _dot((u * u).astype(BF16), wd_ref[...])

    @pl.when(j == pl.num_programs(1) - 1)
    def _():
        o_ref[...] = x_ref[...] + _rms(acc_ref[...], gpost_ref[...])


def _mlp(x2, wu, wd, gpre, gpost):
    tok, d = x2.shape
    ff = wu.shape[1]
    tm = min(512, tok)
    tf = 512
    return pl.pallas_call(
        _mlp_kernel,
        grid=(tok // tm, ff // tf),
        in_specs=[pl.BlockSpec((tm, d), lambda i, j: (i, 0)),
                  pl.BlockSpec((d, tf), lambda i, j: (0, j)),
                  pl.BlockSpec((tf, d), lambda i, j: (j, 0)),
                  pl.BlockSpec((1, d), lambda i, j: (0, 0)), pl.BlockSpec((1, d), lambda i, j: (0, 0))],
        out_specs=pl.BlockSpec((tm, d), lambda i, j: (i, 0)),
        out_shape=jax.ShapeDtypeStruct((tok, d), F32),
        scratch_shapes=[pltpu.VMEM((tm, d), BF16), pltpu.VMEM((tm, d), F32)],
        compiler_params=_cparams("parallel", "arbitrary"),
        name="mlp",
    )(x2, wu, wd, gpre, gpost)


def kernel(x, mem, positions, rel_bias, w_in, w_out, lam_re, lam_im, log_dt, b_re, b_im, c_re, c_im, d_skip, w_glu, lam_q1, lam_k1, lam_q2, lam_k2, g_subln, w_xq, w_xkv, w_xo, w_up, w_down, g_mix_pre, g_mix_post, g_mem, g_x_pre, g_x_post, g_mlp_pre, g_mlp_post):
    b, s, d = x.shape
    tok = b * s
    row = lambda g: g.reshape(1, -1)
    cos2, sin2 = _rot_tables(positions)
    bias_blocks = _bias_blocks(rel_bias)
    x2 = x.reshape(tok, d)
    for l in range(DEPTH):
        lam_init = 0.8 - 0.6 * math.exp(-0.3 * l)
        proj3 = _inproj(x2, row(g_mix_pre[l]), w_in[l].astype(BF16)).reshape(b, s, IN_WIDTH)
        y_ret = _retention(proj3, cos2, sin2)
        y_s5 = _s5(proj3, lam_re[l], lam_im[l], log_dt[l], b_re[l], b_im[l], c_re[l], c_im[l], d_skip[l], w_glu[l])
        y_diff = _diff_attn(proj3, rel_bias, bias_blocks, lam_q1[l], lam_k1[l], lam_q2[l], lam_k2[l],
                            g_subln[l], lam_init)
        x2 = _outproj(y_ret.reshape(tok, -1), y_s5.reshape(tok, -1), y_diff.reshape(tok, -1), x2,
                      w_out[l].astype(BF16), row(g_mix_post[l]))
        kv = _memkv(mem, row(g_mem[l]), w_xkv[l].astype(BF16))
        x2 = _xattn(x2.reshape(b, s, d), kv, w_xq[l].astype(BF16), w_xo[l].astype(BF16),
                    row(g_x_pre[l]), row(g_x_post[l])).reshape(tok, d)
        x2 = _mlp(x2, w_up[l].astype(BF16), w_down[l].astype(BF16), row(g_mlp_pre[l]), row(g_mlp_post[l]))
    return x2.reshape(b, s, d)
```

```python
import functools
import math

import jax
import jax.numpy as jnp
from jax import lax
from jax.experimental import pallas as pl
from jax.experimental.pallas import tpu as pltpu

F32 = jnp.float32
BF16 = jnp.bfloat16

D_MODEL = 2048
DEPTH = 2
MEM_LEN = 256
RET_HEADS = 4
RET_DK = 128
RET_CHUNK = 128
RET_WIDTH = RET_HEADS * RET_DK
S5_WIDTH = 512
S5_GROUP = 16
S5_GROUPS = 32
S5_STATE = 64
S5_NSTATE = S5_GROUPS * S5_STATE
DIFF_HEADS = 8
DIFF_DV = 128
DIFF_DQK = 64
REL_BUCKETS = 32
REL_MAX_DIST = 128
X_HEADS = 4
X_HEAD_DIM = 128
X_WIDTH = X_HEADS * X_HEAD_DIM
D_FF = 4 * D_MODEL
EPS = 1e-6
LOG2E = math.log2(math.e)
IN_WIDTH = 5632
COL_RET_Q, COL_RET_K, COL_RET_V, COL_RET_G, COL_S5_U = 0, 1, 2, 3, 4
COL_DIFF_Q, COL_DIFF_K, COL_DIFF_V = 20, 28, 36

LANES = 128
SUBLANES = 8
VMEM_LIMIT = 56 * 1024 * 1024

NT = (((1,), (1,)), ((), ()))
TN = (((0,), (0,)), ((), ()))


def _cparams(*sem):
    return pltpu.CompilerParams(dimension_semantics=sem, vmem_limit_bytes=VMEM_LIMIT)


def _rms(x, g):
    ms = jnp.mean(x * x, axis=-1, keepdims=True)
    return x * lax.rsqrt(ms + EPS) * g


def _dot(a, b):
    return jnp.dot(a, b, preferred_element_type=F32)


def _inproj_kernel(x_ref, g_ref, w_ref, o_ref, h_ref):
    @pl.when(pl.program_id(1) == 0)
    def _():
        h_ref[...] = _rms(x_ref[...], g_ref[...]).astype(BF16)

    o_ref[...] = _dot(h_ref[...], w_ref[...]).astype(o_ref.dtype)


def _inproj(x2, g, w, l):
    tok, d = x2.shape
    n = w.shape[2]
    tm = min(1024, tok)
    tn = 1408
    return pl.pallas_call(
        _inproj_kernel,
        grid=(tok // tm, n // tn),
        in_specs=[
            pl.BlockSpec((tm, d), lambda i, j: (i, 0)),
            pl.BlockSpec((1, d), lambda i, j: (0, 0)),
            pl.BlockSpec((None, d, tn), lambda i, j: (l, 0, j)),
        ],
        out_specs=pl.BlockSpec((tm, tn), lambda i, j: (i, j)),
        out_shape=jax.ShapeDtypeStruct((tok, n), BF16),
        scratch_shapes=[pltpu.VMEM((tm, d), BF16)],
        compiler_params=_cparams("parallel", "arbitrary"),
        name="inproj",
    )(x2, g, w)


def _rot_kernel(pos_ref, cos_ref, sin_ref):
    pos = pos_ref[...].astype(F32)
    lane = lax.broadcasted_iota(jnp.int32, (1, LANES), 1)
    half = RET_DK // 2
    j = jnp.where(lane < half, lane, lane - half).astype(F32)
    inv = jnp.exp(j * (-math.log(10000.0) / half))
    ang = pos * inv
    cos_ref[...] = jnp.cos(ang)
    s = jnp.sin(ang)
    sin_ref[...] = jnp.where(lane < half, -s, s)


def _rot_tables(positions):
    b, s = positions.shape
    out = jax.ShapeDtypeStruct((b, s, LANES), F32)
    return pl.pallas_call(
        _rot_kernel,
        grid=(b,),
        in_specs=[pl.BlockSpec((None, s, 1), lambda i: (i, 0, 0))],
        out_specs=[pl.BlockSpec((None, s, LANES), lambda i: (i, 0, 0))] * 2,
        out_shape=[out, out],
        compiler_params=_cparams("parallel"),
        name="rot_tables",
    )(positions.reshape(b, s, 1))


RET_TBLK = 512


def _ret_kernel(q_ref, k_ref, v_ref, g_ref, cos_ref, sin_ref, dec_ref, xi_ref, zeta_ref, gc_ref, o_ref, r_ref):
    c_len = RET_CHUNK
    scale = RET_DK ** -0.5

    @pl.when(pl.program_id(1) == 0)
    def _():
        r_ref[...] = jnp.zeros_like(r_ref)

    for c in range(q_ref.shape[0] // c_len):
        rows = slice(c * c_len, (c + 1) * c_len)
        cs = cos_ref[rows, :]
        sn = sin_ref[rows, :]
        for h in range(RET_HEADS):
            cols = slice(h * RET_DK, (h + 1) * RET_DK)
            q = q_ref[rows, cols].astype(F32)
            k = k_ref[rows, cols].astype(F32)
            qr = q * cs + pltpu.roll(q, RET_DK // 2, 1) * sn
            kr = (k * cs + pltpu.roll(k, RET_DK // 2, 1) * sn) * scale
            vb = v_ref[rows, cols]
            r_state = r_ref[h]
            inner = lax.dot_general(qr.astype(BF16), kr.astype(BF16), NT, preferred_element_type=F32) * dec_ref[h]
            lhs = jnp.concatenate([inner.astype(BF16), (qr * xi_ref[h]).astype(BF16)], axis=1)
            rhs = jnp.concatenate([vb, r_state.astype(BF16)], axis=0)
            o = _dot(lhs, rhs)
            r_ref[h] = gc_ref[h] * r_state + lax.dot_general((kr * zeta_ref[h]).astype(BF16), vb, TN,
                                                             preferred_element_type=F32)
            o = o * lax.rsqrt(jnp.mean(o * o, axis=-1, keepdims=True) + EPS)
            gate = g_ref[rows, cols].astype(F32)
            o_ref[rows, cols] = (o * (gate * jax.nn.sigmoid(gate))).astype(o_ref.dtype)


def _ret_consts():
    h, c = RET_HEADS, RET_CHUNK
    log_g = jnp.log(1.0 - 2.0 ** (-5.0 - jnp.arange(h, dtype=F32)))
    idx = jnp.arange(c, dtype=F32)
    dist = idx[:, None] - idx[None, :]
    decay = jnp.where(dist >= 0, jnp.exp(jnp.maximum(dist, 0.0)[None] * log_g[:, None, None]), 0.0)
    xi = jnp.exp((idx + 1.0)[None, :] * log_g[:, None])
    zeta = jnp.exp((c - 1.0 - idx)[None, :] * log_g[:, None])
    g_chunk = jnp.exp(c * log_g)
    bc = lambda t: jnp.broadcast_to(t[:, :, None], (h, c, LANES))
    gcb = jnp.broadcast_to(g_chunk[:, None, None], (h, c, LANES))
    return decay, bc(xi), bc(zeta), gcb


def _retention(proj3, cos2, sin2):
    b, s, _ = proj3.shape
    tb = min(RET_TBLK, s)
    decay, xi, zeta, gcb = _ret_consts()
    tok_blk = lambda col: pl.BlockSpec((None, tb, RET_WIDTH), lambda i, t: (i, t, col))
    pos_blk = pl.BlockSpec((None, tb, LANES), lambda i, t: (i, t, 0))
    const = pl.BlockSpec((RET_HEADS, RET_CHUNK, LANES), lambda i, t: (0, 0, 0))
    return pl.pallas_call(
        _ret_kernel,
        grid=(b, s // tb),
        in_specs=[tok_blk(COL_RET_Q), tok_blk(COL_RET_K), tok_blk(COL_RET_V), tok_blk(COL_RET_G),
                  pos_blk, pos_blk, const, const, const, const],
        out_specs=pl.BlockSpec((None, tb, RET_WIDTH), lambda i, t: (i, t, 0)),
        out_shape=jax.ShapeDtypeStruct((b, s, RET_WIDTH), BF16),
        scratch_shapes=[pltpu.VMEM((RET_HEADS, RET_DK, RET_DK), F32)],
        compiler_params=_cparams("parallel", "arbitrary"),
        name="retention",
    )(proj3, proj3, proj3, proj3, cos2, sin2, decay, xi, zeta, gcb)


def _s5prep_kernel(lr_ref, li_ref, ldt_ref, br_ref, bi_ref, ar_ref, ai_ref, bbr_ref, bbi_ref):
    lr = jnp.minimum(lr_ref[...], -1e-4)
    li = li_ref[...]
    dt = jnp.exp(ldt_ref[...])
    mag = jnp.exp(lr * dt)
    ar = mag * jnp.cos(li * dt)
    ai = mag * jnp.sin(li * dt)
    nr, ni = ar - 1.0, ai
    den = lr * lr + li * li
    fr = (nr * lr + ni * li) / den
    fi = (ni * lr - nr * li) / den
    br = br_ref[...]
    bi = bi_ref[...]
    bbr_ref[...] = fr[None] * br - fi[None] * bi
    bbi_ref[...] = fr[None] * bi + fi[None] * br
    ar_ref[...] = ar
    ai_ref[...] = ai


def _s5_prep(lam_re, lam_im, log_dt, b_re, b_im):
    g, p = lam_re.shape
    ab = jax.ShapeDtypeStruct((g, p), F32)
    bb = jax.ShapeDtypeStruct((S5_GROUP, g, p), F32)
    return pl.pallas_call(
        _s5prep_kernel,
        out_shape=[ab, ab, bb, bb],
        name="s5_prep",
    )(lam_re, lam_im, log_dt.reshape(g, 1), b_re.transpose(2, 0, 1), b_im.transpose(2, 0, 1))


S5_SCAN_LANES = 512
S5_TSTEPS = 32


def _s5_kernel(u_ref, bcat_ref, cre_ref, cim_ref, ar_ref, ai_ref, dskip_ref, wglu_ref, o_ref, bu_ref, st_ref, *, nb):
    n = S5_NSTATE
    t_steps = u_ref.shape[0] // nb

    @pl.when(pl.program_id(0) == 0)
    def _():
        st_ref[...] = jnp.zeros_like(st_ref)

    u = u_ref[...]
    hw = S5_WIDTH // 2
    hn = n // 2
    for part in range(2):
        for half in range(2):
            cols = slice(part * n + half * hn, part * n + (half + 1) * hn)
            bu_ref[:, cols] = _dot(u[:, half * hw:(half + 1) * hw], bcat_ref[half * hw:(half + 1) * hw, cols])

    for lo in range(0, n, S5_SCAN_LANES):
        sl_r = slice(lo, lo + S5_SCAN_LANES)
        sl_i = slice(n + lo, n + lo + S5_SCAN_LANES)
        a_r = jnp.broadcast_to(ar_ref[:, sl_r], (nb, S5_SCAN_LANES))
        a_i = jnp.broadcast_to(ai_ref[:, sl_r], (nb, S5_SCAN_LANES))

        def step(t, carry, sl_r=sl_r, sl_i=sl_i, a_r=a_r, a_i=a_i):
            xr, xi = carry
            rs = pl.ds(pl.multiple_of(t * nb, nb), nb)
            nr = a_r * xr - a_i * xi + bu_ref[rs, sl_r]
            ni = a_r * xi + a_i * xr + bu_ref[rs, sl_i]
            bu_ref[rs, sl_r] = nr
            bu_ref[rs, sl_i] = ni
            return nr, ni

        xr, xi = lax.fori_loop(0, t_steps, step, (st_ref[:, sl_r], st_ref[:, sl_i]), unroll=2)
        st_ref[:, sl_r] = xr
        st_ref[:, sl_i] = xi

    ys = []
    for half in range(2):
        xr = bu_ref[:, half * hn:(half + 1) * hn].astype(BF16)
        xi = bu_ref[:, n + half * hn:n + (half + 1) * hn].astype(BF16)
        rows = slice(half * hn, (half + 1) * hn)
        cols = slice(half * hw, (half + 1) * hw)
        ys.append(_dot(xr, cre_ref[rows, cols]) - _dot(xi, cim_ref[rows, cols]))
    y = jnp.concatenate(ys, axis=1) + dskip_ref[...] * u.astype(F32)
    gl = jax.nn.gelu(y)
    o_ref[...] = (gl * jax.nn.sigmoid(_dot(gl.astype(BF16), wglu_ref[...]))).astype(o_ref.dtype)


def _s5(proj3, lam_re, lam_im, log_dt, b_re, b_im, c_re, c_im, d_skip, w_glu):
    b, s, _ = proj3.shape
    g, hh = S5_GROUPS, S5_GROUP
    n = S5_NSTATE
    a_r, a_i, bb_r, bb_i = _s5_prep(lam_re, lam_im, log_dt, b_re, b_im)
    eye = jnp.eye(g, dtype=F32)
    blk_in = lambda t: jnp.einsum('hgp,gk->ghkp', t, eye).reshape(g * hh, n)
    blk_out = lambda t: jnp.einsum('ghp,gk->gpkh', t, eye).reshape(n, g * hh)
    bcat = jnp.concatenate([blk_in(bb_r), blk_in(bb_i)], axis=1).astype(BF16)
    cre = blk_out(c_re).astype(BF16)
    cim = blk_out(c_im).astype(BF16)
    u_tb = proj3[:, :, COL_S5_U * S5_WIDTH:(COL_S5_U + 1) * S5_WIDTH].transpose(1, 0, 2).reshape(s * b, S5_WIDTH)
    tm = b * min(S5_TSTEPS, s)
    full = lambda shape: pl.BlockSpec(shape, lambda t: (0,) * len(shape))
    y_tb = pl.pallas_call(
        functools.partial(_s5_kernel, nb=b),
        grid=(s * b // tm,),
        in_specs=[pl.BlockSpec((tm, S5_WIDTH), lambda t: (t, 0)),
                  full((S5_WIDTH, 2 * n)), full((n, S5_WIDTH)), full((n, S5_WIDTH)),
                  full((1, n)), full((1, n)), full((1, S5_WIDTH)), full((S5_WIDTH, S5_WIDTH))],
        out_specs=pl.BlockSpec((tm, S5_WIDTH), lambda t: (t, 0)),
        out_shape=jax.ShapeDtypeStruct((s * b, S5_WIDTH), BF16),
        scratch_shapes=[pltpu.VMEM((tm, 2 * n), F32), pltpu.VMEM((b, 2 * n), F32)],
        compiler_params=_cparams("arbitrary"),
        name="s5",
    )(u_tb, bcat, cre, cim, a_r.reshape(1, n), a_i.reshape(1, n), d_skip.reshape(1, S5_WIDTH), w_glu.astype(BF16))
    return y_tb.reshape(s, b, S5_WIDTH).transpose(1, 0, 2)


ATT_TILE = 256


def _bias_kernel(tbl_ref, o_ref):
    h = pl.program_id(0)
    t = ATT_TILE
    r = lax.broadcasted_iota(jnp.int32, (t, t), 0)
    c = lax.broadcasted_iota(jnp.int32, (t, t), 1)
    max_exact = REL_BUCKETS // 2
    far = tbl_ref[REL_BUCKETS - 1, h]
    for d in range(2):
        n = jnp.maximum(r - c + t * d, 0)
        large = max_exact + (jnp.log(jnp.maximum(n, 1).astype(F32) / max_exact)
                             / math.log(REL_MAX_DIST / max_exact) * (REL_BUCKETS - max_exact)).astype(jnp.int32)
        large = jnp.minimum(large, REL_BUCKETS - 1)
        bucket = jnp.where(n < max_exact, n, large)
        acc = jnp.zeros((t, t), F32)
        for bk in range(REL_BUCKETS):
            acc = jnp.where(bucket == bk, tbl_ref[bk, h], acc)
        o_ref[d] = (acc - far) * LOG2E


def _bias_blocks(rel_bias):
    t = ATT_TILE
    return pl.pallas_call(
        _bias_kernel,
        grid=(DIFF_HEADS,),
        in_specs=[pl.BlockSpec(memory_space=pltpu.SMEM)],
        out_specs=pl.BlockSpec((None, 2, t, t), lambda h: (h, 0, 0, 0)),
        out_shape=jax.ShapeDtypeStruct((DIFF_HEADS, 2, t, t), F32),
        compiler_params=_cparams("arbitrary"),
        name="t5_bias_blocks",
    )(rel_bias)


def _diff_kernel(q_ref, k_ref, v_ref, bias_ref, lq1_ref, lk1_ref, lq2_ref, lk2_ref, gs_ref, o_ref, s_ref, *, lam_init):
    t = ATT_TILE
    n_tiles = q_ref.shape[0] // t
    lam = (jnp.exp(jnp.sum(lq1_ref[...] * lk1_ref[...], axis=-1, keepdims=True))
           - jnp.exp(jnp.sum(lq2_ref[...] * lk2_ref[...], axis=-1, keepdims=True)) + lam_init)
    lane_q = lax.broadcasted_iota(jnp.int32, (t, LANES), 1)
    col = lax.broadcasted_iota(jnp.int32, (2 * t, t), 1)
    row = lax.broadcasted_iota(jnp.int32, (2 * t, t), 0)
    causal2 = col <= jnp.where(row < t, row, row - t)
    gs = gs_ref[...] * (1.0 - lam_init)

    for i in range(n_tiles):
        q = q_ref[i * t:(i + 1) * t, :].astype(F32) * (DIFF_DQK ** -0.5 * LOG2E)
        q2 = jnp.concatenate([jnp.where(lane_q < DIFF_DQK, q, 0.0), jnp.where(lane_q >= DIFF_DQK, q, 0.0)],
                             axis=0).astype(BF16)
        m_acc = None
        for j in range(i + 1):
            s = lax.dot_general(q2, k_ref[j * t:(j + 1) * t, :], NT, preferred_element_type=F32)
            if j == i:
                b0 = bias_ref[0]
                s = jnp.where(causal2, s + jnp.concatenate([b0, b0], axis=0), -jnp.inf)
            elif j == i - 1:
                b1 = bias_ref[1]
                s = s + jnp.concatenate([b1, b1], axis=0)
            s_ref[j] = s
            mj = jnp.maximum(s[:, :LANES], s[:, LANES:])
            m_acc = mj if m_acc is None else jnp.maximum(m_acc, mj)
        m = jnp.max(m_acc, axis=-1, keepdims=True)
        l_acc = jnp.zeros((2 * t, LANES), F32)
        o2 = jnp.zeros((2 * t, DIFF_DV), F32)
        for j in range(i + 1):
            e = jnp.exp2(s_ref[j] - m)
            l_acc = l_acc + (e[:, :LANES] + e[:, LANES:])
            o2 = o2 + _dot(e.astype(BF16), v_ref[j * t:(j + 1) * t, :])
        l_sum = jnp.sum(l_acc, axis=-1, keepdims=True)
        o = o2[:t] * (1.0 / l_sum[:t]) - o2[t:] * (lam / l_sum[t:])
        o = o * lax.rsqrt(jnp.mean(o * o, axis=-1, keepdims=True) + EPS) * gs
        o_ref[i * t:(i + 1) * t, :] = o.astype(o_ref.dtype)


def _diff_attn(proj3, bias_blocks, lq1, lk1, lq2, lk2, g_subln, lam_init):
    b, s, _ = proj3.shape
    t = ATT_TILE
    tok_blk = lambda col: pl.BlockSpec((None, s, LANES), lambda i, h: (i, 0, col + h))
    vec = lambda n: pl.BlockSpec((1, n), lambda i, h: (0, 0))
    return pl.pallas_call(
        functools.partial(_diff_kernel, lam_init=lam_init),
        grid=(b, DIFF_HEADS),
        in_specs=[tok_blk(COL_DIFF_Q), tok_blk(COL_DIFF_K), tok_blk(COL_DIFF_V),
                  pl.BlockSpec((None, 2, t, t), lambda i, h: (h, 0, 0, 0)),
                  vec(DIFF_DQK), vec(DIFF_DQK), vec(DIFF_DQK), vec(DIFF_DQK), vec(DIFF_DV)],
        out_specs=pl.BlockSpec((None, s, LANES), lambda i, h: (i, 0, h)),
        out_shape=jax.ShapeDtypeStruct((b, s, DIFF_HEADS * DIFF_DV), BF16),
        scratch_shapes=[pltpu.VMEM((s // t, 2 * t, t), F32)],
        compiler_params=_cparams("parallel", "parallel"),
        name="diff_attn",
    )(proj3, proj3, proj3, bias_blocks,
      lq1.reshape(1, -1), lk1.reshape(1, -1), lq2.reshape(1, -1), lk2.reshape(1, -1), g_subln.reshape(1, -1))


def _outproj_kernel(yr_ref, ys_ref, yd_ref, x_ref, w_ref, g_ref, o_ref):
    n_r = yr_ref.shape[1]
    n_s = ys_ref.shape[1]
    acc = _dot(yr_ref[...], w_ref[:n_r, :])
    acc = acc + _dot(ys_ref[...], w_ref[n_r:n_r + n_s, :])
    acc = acc + _dot(yd_ref[...], w_ref[n_r + n_s:, :])
    o_ref[...] = x_ref[...] + _rms(acc, g_ref[...])


def _outproj(y_ret, y_s5, y_diff, x2, w, g, l):
    tok, d = x2.shape
    tm = min(512, tok)
    row = lambda n: pl.BlockSpec((tm, n), lambda i: (i, 0))
    return pl.pallas_call(
        _outproj_kernel,
        grid=(tok // tm,),
        in_specs=[row(y_ret.shape[1]), row(y_s5.shape[1]), row(y_diff.shape[1]), row(d),
                  pl.BlockSpec((None, d, d), lambda i: (l, 0, 0)), pl.BlockSpec((1, d), lambda i: (0, 0))],
        out_specs=row(d),
        out_shape=jax.ShapeDtypeStruct((tok, d), F32),
        compiler_params=_cparams("parallel"),
        name="outproj",
    )(y_ret, y_s5, y_diff, x2, w, g)


def _memkv_kernel(m_ref, g_ref, w_ref, o_ref):
    o_ref[...] = _dot(_rms(m_ref[...], g_ref[...]).astype(BF16), w_ref[...]).astype(o_ref.dtype)


def _memkv(mem, g, w, l):
    b, m, d = mem.shape
    n = w.shape[2]
    return pl.pallas_call(
        _memkv_kernel,
        grid=(b,),
        in_specs=[pl.BlockSpec((None, m, d), lambda i: (i, 0, 0)),
                  pl.BlockSpec((1, d), lambda i: (0, 0)), pl.BlockSpec((None, d, n), lambda i: (l, 0, 0))],
        out_specs=pl.BlockSpec((None, m, n), lambda i: (i, 0, 0)),
        out_shape=jax.ShapeDtypeStruct((b, m, n), BF16),
        compiler_params=_cparams("parallel"),
        name="mem_kv",
    )(mem, g, w)


def _xattn_kernel(x_ref, kv_ref, wq_ref, wo_ref, gpre_ref, gpost_ref, o_ref):
    x = x_ref[...]
    hn = _rms(x, gpre_ref[...]).astype(BF16)
    q = (_dot(hn, wq_ref[...]) * (X_HEAD_DIM ** -0.5)).astype(BF16)
    outs = []
    for hd in range(X_HEADS):
        lo = hd * X_HEAD_DIM
        kh = kv_ref[:, lo:lo + X_HEAD_DIM]
        vh = kv_ref[:, X_WIDTH + lo:X_WIDTH + lo + X_HEAD_DIM]
        s = lax.dot_general(q[:, lo:lo + X_HEAD_DIM], kh, NT, preferred_element_type=F32)
        e = jnp.exp(s - jnp.max(s, axis=-1, keepdims=True))
        l_sum = jnp.sum(e, axis=-1, keepdims=True)
        outs.append(_dot(e.astype(BF16), vh) / l_sum)
    o = jnp.concatenate(outs, axis=-1).astype(BF16)
    o_ref[...] = x + _rms(_dot(o, wo_ref[...]), gpost_ref[...])


def _xattn(x3, kv, wq, wo, gpre, gpost, l):
    b, s, d = x3.shape
    m = kv.shape[1]
    tq = min(512, s)
    full = lambda shape: pl.BlockSpec(shape, lambda i, t: (0,) * len(shape))
    return pl.pallas_call(
        _xattn_kernel,
        grid=(b, s // tq),
        in_specs=[pl.BlockSpec((None, tq, d), lambda i, t: (i, t, 0)),
                  pl.BlockSpec((None, m, 2 * X_WIDTH), lambda i, t: (i, 0, 0)),
                  pl.BlockSpec((None, d, X_WIDTH), lambda i, t: (l, 0, 0)),
                  pl.BlockSpec((None, X_WIDTH, d), lambda i, t: (l, 0, 0)), full((1, d)), full((1, d))],
        out_specs=pl.BlockSpec((None, tq, d), lambda i, t: (i, t, 0)),
        out_shape=jax.ShapeDtypeStruct((b, s, d), F32),
        compiler_params=_cparams("parallel", "parallel"),
        name="xattn",
    )(x3, kv, wq, wo, gpre, gpost)


def _mlp_kernel(x_ref, wu_ref, wd_ref, gpre_ref, gpost_ref, o_ref, h_ref, acc_ref):
    j = pl.program_id(1)

    @pl.when(j == 0)
    def _():
        h_ref[...] = _rms(x_ref[...], gpre_ref[...]).astype(BF16)
        acc_ref[...] = jnp.zeros_like(acc_ref)

    u = jnp.maximum(_dot(h_ref[...], wu_ref[...]), 0.0)
    acc_ref[...] += _dot((u * u).astype(BF16), wd_ref[...])

    @pl.when(j == pl.num_programs(1) - 1)
    def _():
        o_ref[...] = x_ref[...] + _rms(acc_ref[...], gpost_ref[...])


def _mlp(x2, wu, wd, gpre, gpost, l):
    tok, d = x2.shape
    ff = wu.shape[2]
    tm = min(512, tok)
    tf = 1024
    return pl.pallas_call(
        _mlp_kernel,
        grid=(tok // tm, ff // tf),
        in_specs=[pl.BlockSpec((tm, d), lambda i, j: (i, 0)),
                  pl.BlockSpec((None, d, tf), lambda i, j: (l, 0, j)),
                  pl.BlockSpec((None, tf, d), lambda i, j: (l, j, 0)),
                  pl.BlockSpec((1, d), lambda i, j: (0, 0)), pl.BlockSpec((1, d), lambda i, j: (0, 0))],
        out_specs=pl.BlockSpec((tm, d), lambda i, j: (i, 0)),
        out_shape=jax.ShapeDtypeStruct((tok, d), F32),
        scratch_shapes=[pltpu.VMEM((tm, d), BF16), pltpu.VMEM((tm, d), F32)],
        compiler_params=_cparams("parallel", "arbitrary"),
        name="mlp",
    )(x2, wu, wd, gpre, gpost)


def kernel(x, mem, positions, rel_bias, w_in, w_out, lam_re, lam_im, log_dt, b_re, b_im, c_re, c_im, d_skip, w_glu, lam_q1, lam_k1, lam_q2, lam_k2, g_subln, w_xq, w_xkv, w_xo, w_up, w_down, g_mix_pre, g_mix_post, g_mem, g_x_pre, g_x_post, g_mlp_pre, g_mlp_post):
    b, s, d = x.shape
    tok = b * s
    row = lambda g: g.reshape(1, -1)
    cos2, sin2 = _rot_tables(positions)
    bias_blocks = _bias_blocks(rel_bias)
    w_in, w_out, w_xq, w_xkv, w_xo, w_up, w_down = (
        w.astype(BF16) for w in (w_in, w_out, w_xq, w_xkv, w_xo, w_up, w_down))
    x2 = x.reshape(tok, d)
    for l in range(DEPTH):
        lam_init = 0.8 - 0.6 * math.exp(-0.3 * l)
        proj3 = _inproj(x2, row(g_mix_pre[l]), w_in, l).reshape(b, s, IN_WIDTH)
        y_ret = _retention(proj3, cos2, sin2)
        y_s5 = _s5(proj3, lam_re[l], lam_im[l], log_dt[l], b_re[l], b_im[l], c_re[l], c_im[l], d_skip[l], w_glu[l])
        y_diff = _diff_attn(proj3, bias_blocks, lam_q1[l], lam_k1[l], lam_q2[l], lam_k2[l], g_subln[l], lam_init)
        x2 = _outproj(y_ret.reshape(tok, -1), y_s5.reshape(tok, -1), y_diff.reshape(tok, -1), x2,
                      w_out, row(g_mix_post[l]), l)
        kv = _memkv(mem, row(g_mem[l]), w_xkv, l)
        x2 = _xattn(x2.reshape(b, s, d), kv, w_xq, w_xo, row(g_x_pre[l]), row(g_x_post[l]), l).reshape(tok, d)
        x2 = _mlp(x2, w_up, w_down, row(g_mlp_pre[l]), row(g_mlp_post[l]), l)
    return x2.reshape(b, s, d)
```

```python
import functools
import math

import jax
import jax.numpy as jnp
from jax import lax
from jax.experimental import pallas as pl
from jax.experimental.pallas import tpu as pltpu

F32 = jnp.float32
BF16 = jnp.bfloat16

D_MODEL = 2048
DEPTH = 2
MEM_LEN = 256
RET_HEADS = 4
RET_DK = 128
RET_CHUNK = 128
RET_WIDTH = RET_HEADS * RET_DK
S5_WIDTH = 512
S5_GROUP = 16
S5_GROUPS = 32
S5_STATE = 64
S5_NSTATE = S5_GROUPS * S5_STATE
DIFF_HEADS = 8
DIFF_DV = 128
DIFF_DQK = 64
REL_BUCKETS = 32
REL_MAX_DIST = 128
X_HEADS = 4
X_HEAD_DIM = 128
X_WIDTH = X_HEADS * X_HEAD_DIM
D_FF = 4 * D_MODEL
EPS = 1e-6
LOG2E = math.log2(math.e)
IN_WIDTH = 5632
COL_RET_Q, COL_RET_K, COL_RET_V, COL_RET_G, COL_S5_U = 0, 1, 2, 3, 4
COL_DIFF_Q, COL_DIFF_K, COL_DIFF_V = 20, 28, 36

LANES = 128
SUBLANES = 8
VMEM_LIMIT = 56 * 1024 * 1024

NT = (((1,), (1,)), ((), ()))
TN = (((0,), (0,)), ((), ()))


def _cparams(*sem):
    return pltpu.CompilerParams(dimension_semantics=sem, vmem_limit_bytes=VMEM_LIMIT)


def _rms(x, g):
    ms = jnp.mean(x * x, axis=-1, keepdims=True)
    return x * lax.rsqrt(ms + EPS) * g


def _dot(a, b):
    return jnp.dot(a, b, preferred_element_type=F32)


def _inproj_kernel(x_ref, g_ref, w_ref, o_ref):
    o_ref[...] = _dot(_rms(x_ref[...], g_ref[...]).astype(BF16), w_ref[...]).astype(o_ref.dtype)


def _inproj(x2, g, w, l):
    tok, d = x2.shape
    n = w.shape[2]
    tm = min(512, tok)
    return pl.pallas_call(
        _inproj_kernel,
        grid=(tok // tm,),
        in_specs=[
            pl.BlockSpec((tm, d), lambda i: (i, 0)),
            pl.BlockSpec((1, d), lambda i: (0, 0)),
            pl.BlockSpec((None, d, n), lambda i: (l, 0, 0), pipeline_mode=pl.Buffered(1)),
        ],
        out_specs=pl.BlockSpec((tm, n), lambda i: (i, 0)),
        out_shape=jax.ShapeDtypeStruct((tok, n), BF16),
        compiler_params=_cparams("parallel"),
        name="inproj",
    )(x2, g, w)


def _rot_kernel(pos_ref, cos_ref, sin_ref):
    hs = pos_ref.shape[0] // 2
    half = RET_DK // 2
    lane = lax.broadcasted_iota(jnp.int32, (1, LANES), 1)
    lo = lane < half
    j = jnp.where(lo, lane, lane - half).astype(F32)
    inv = jnp.exp(j * (-math.log(10000.0) / half))
    pos = pos_ref[...].astype(F32)
    ang = jnp.where(lo, pos[:hs], pos[hs:]) * inv
    c = jnp.cos(ang)
    s = jnp.sin(ang)
    c_sw = pltpu.roll(c, half, 1)
    s_sw = pltpu.roll(s, half, 1)
    cos_ref[:hs] = jnp.where(lo, c, c_sw)
    cos_ref[hs:] = jnp.where(lo, c_sw, c)
    sin_ref[:hs] = jnp.where(lo, -s, s_sw)
    sin_ref[hs:] = jnp.where(lo, -s_sw, s)


def _rot_tables(positions):
    b, s = positions.shape
    out = jax.ShapeDtypeStruct((b, s, LANES), F32)
    return pl.pallas_call(
        _rot_kernel,
        grid=(b,),
        in_specs=[pl.BlockSpec((None, s, 1), lambda i: (i, 0, 0))],
        out_specs=[pl.BlockSpec((None, s, LANES), lambda i: (i, 0, 0))] * 2,
        out_shape=[out, out],
        compiler_params=_cparams("parallel"),
        name="rot_tables",
    )(positions.reshape(b, s, 1))


RET_TBLK = 512


def _ret_kernel(q_ref, k_ref, v_ref, g_ref, cos_ref, sin_ref, dec_ref, xi_ref, zeta_ref, gc_ref, o_ref, r_ref):
    c_len = RET_CHUNK
    scale = RET_DK ** -0.5

    @pl.when(pl.program_id(1) == 0)
    def _():
        r_ref[...] = jnp.zeros_like(r_ref)

    n_c = q_ref.shape[0] // c_len
    stage = []
    for c in range(n_c):
        rows = slice(c * c_len, (c + 1) * c_len)
        cs = cos_ref[rows, :]
        sn = sin_ref[rows, :]
        for h in range(RET_HEADS):
            cols = slice(h * RET_DK, (h + 1) * RET_DK)
            q = q_ref[rows, cols].astype(F32)
            k = k_ref[rows, cols].astype(F32)
            qr = q * cs + pltpu.roll(q, RET_DK // 2, 1) * sn
            kr = (k * cs + pltpu.roll(k, RET_DK // 2, 1) * sn) * scale
            vb = v_ref[rows, cols]
            inner = lax.dot_general(qr.astype(BF16), kr.astype(BF16), NT, preferred_element_type=F32) * dec_ref[h]
            d_r = lax.dot_general((kr * zeta_ref[h]).astype(BF16), vb, TN, preferred_element_type=F32)
            lhs = jnp.concatenate([inner.astype(BF16), (qr * xi_ref[h]).astype(BF16)], axis=1)
            stage.append((lhs, vb, d_r))
    r_states = [r_ref[h] for h in range(RET_HEADS)]
    for c in range(n_c):
        rows = slice(c * c_len, (c + 1) * c_len)
        for h in range(RET_HEADS):
            cols = slice(h * RET_DK, (h + 1) * RET_DK)
            lhs, vb, d_r = stage[c * RET_HEADS + h]
            o = _dot(lhs, jnp.concatenate([vb, r_states[h].astype(BF16)], axis=0))
            r_states[h] = gc_ref[h] * r_states[h] + d_r
            o = o * lax.rsqrt(jnp.mean(o * o, axis=-1, keepdims=True) + EPS)
            gate = g_ref[rows, cols].astype(F32)
            o_ref[rows, cols] = (o * (gate * jax.nn.sigmoid(gate))).astype(o_ref.dtype)
    for h in range(RET_HEADS):
        r_ref[h] = r_states[h]


def _ret_consts():
    h, c = RET_HEADS, RET_CHUNK
    log_g = jnp.log(1.0 - 2.0 ** (-5.0 - jnp.arange(h, dtype=F32)))
    idx = jnp.arange(c, dtype=F32)
    dist = idx[:, None] - idx[None, :]
    decay = jnp.where(dist >= 0, jnp.exp(jnp.maximum(dist, 0.0)[None] * log_g[:, None, None]), 0.0)
    xi = jnp.exp((idx + 1.0)[None, :] * log_g[:, None])
    zeta = jnp.exp((c - 1.0 - idx)[None, :] * log_g[:, None])
    g_chunk = jnp.exp(c * log_g)
    bc = lambda t: jnp.broadcast_to(t[:, :, None], (h, c, LANES))
    gcb = jnp.broadcast_to(g_chunk[:, None, None], (h, c, LANES))
    return decay, bc(xi), bc(zeta), gcb


def _retention(proj3, cos2, sin2):
    b, s, _ = proj3.shape
    tb = min(RET_TBLK, s)
    decay, xi, zeta, gcb = _ret_consts()
    tok_blk = lambda col: pl.BlockSpec((None, tb, RET_WIDTH), lambda i, t: (i, t, col))
    pos_blk = pl.BlockSpec((None, tb, LANES), lambda i, t: (i, t, 0))
    const = pl.BlockSpec((RET_HEADS, RET_CHUNK, LANES), lambda i, t: (0, 0, 0))
    return pl.pallas_call(
        _ret_kernel,
        grid=(b, s // tb),
        in_specs=[tok_blk(COL_RET_Q), tok_blk(COL_RET_K), tok_blk(COL_RET_V), tok_blk(COL_RET_G),
                  pos_blk, pos_blk, const, const, const, const],
        out_specs=pl.BlockSpec((None, tb, RET_WIDTH), lambda i, t: (i, t, 0)),
        out_shape=jax.ShapeDtypeStruct((b, s, RET_WIDTH), BF16),
        scratch_shapes=[pltpu.VMEM((RET_HEADS, RET_DK, RET_DK), F32)],
        compiler_params=_cparams("parallel", "arbitrary"),
        name="retention",
    )(proj3, proj3, proj3, proj3, cos2, sin2, decay, xi, zeta, gcb)


def _s5prep_kernel(lr_ref, li_ref, ldt_ref, br_ref, bi_ref, ar_ref, ai_ref, bbr_ref, bbi_ref):
    lr = jnp.minimum(lr_ref[...], -1e-4)
    li = li_ref[...]
    dt = jnp.exp(ldt_ref[...])
    mag = jnp.exp(lr * dt)
    ar = mag * jnp.cos(li * dt)
    ai = mag * jnp.sin(li * dt)
    nr, ni = ar - 1.0, ai
    den = lr * lr + li * li
    fr = (nr * lr + ni * li) / den
    fi = (ni * lr - nr * li) / den
    br = br_ref[...]
    bi = bi_ref[...]
    bbr_ref[...] = fr[None] * br - fi[None] * bi
    bbi_ref[...] = fr[None] * bi + fi[None] * br
    ar_ref[...] = ar
    ai_ref[...] = ai


def _s5_prep(lam_re, lam_im, log_dt, b_re, b_im):
    g, p = lam_re.shape
    ab = jax.ShapeDtypeStruct((g, p), F32)
    bb = jax.ShapeDtypeStruct((S5_GROUP, g, p), F32)
    return pl.pallas_call(
        _s5prep_kernel,
        out_shape=[ab, ab, bb, bb],
        name="s5_prep",
    )(lam_re, lam_im, log_dt.reshape(g, 1), b_re.transpose(2, 0, 1), b_im.transpose(2, 0, 1))


S5_SCAN_LANES = 512
S5_TSTEPS = 32


def _s5_kernel(u_ref, perm_ref, perm_t_ref, bcat_ref, cre_ref, cim_ref, ar_ref, ai_ref, dskip_ref, wglu_ref, o_ref,
               bu_ref, st_ref):
    n = S5_NSTATE
    nb, t_steps, _ = u_ref.shape

    @pl.when(pl.program_id(0) == 0)
    def _():
        st_ref[...] = jnp.zeros_like(st_ref)

    u = _dot(perm_ref[...], u_ref[...].reshape(nb * t_steps, S5_WIDTH)).astype(BF16)
    hw = S5_WIDTH // 2
    hn = n // 2
    for part in range(2):
        for half in range(2):
            cols = slice(part * n + half * hn, part * n + (half + 1) * hn)
            bu_ref[:, cols] = _dot(u[:, half * hw:(half + 1) * hw], bcat_ref[half * hw:(half + 1) * hw, cols])

    for lo in range(0, n, S5_SCAN_LANES):
        sl_r = slice(lo, lo + S5_SCAN_LANES)
        sl_i = slice(n + lo, n + lo + S5_SCAN_LANES)
        a_r = jnp.broadcast_to(ar_ref[:, sl_r], (nb, S5_SCAN_LANES))
        a_i = jnp.broadcast_to(ai_ref[:, sl_r], (nb, S5_SCAN_LANES))

        xr = st_ref[:, sl_r]
        xi = st_ref[:, sl_i]
        for t in range(t_steps):
            rs = slice(t * nb, (t + 1) * nb)
            xr, xi = (a_r * xr - a_i * xi + bu_ref[rs, sl_r],
                      a_r * xi + a_i * xr + bu_ref[rs, sl_i])
            bu_ref[rs, sl_r] = xr
            bu_ref[rs, sl_i] = xi
        st_ref[:, sl_r] = xr
        st_ref[:, sl_i] = xi

    ys = []
    for half in range(2):
        xr = bu_ref[:, half * hn:(half + 1) * hn].astype(BF16)
        xi = bu_ref[:, n + half * hn:n + (half + 1) * hn].astype(BF16)
        rows = slice(half * hn, (half + 1) * hn)
        cols = slice(half * hw, (half + 1) * hw)
        ys.append(_dot(xr, cre_ref[rows, cols]) - _dot(xi, cim_ref[rows, cols]))
    y = jnp.concatenate(ys, axis=1) + dskip_ref[...] * u.astype(F32)
    gl = jax.nn.gelu(y)
    y_tb = (gl * jax.nn.sigmoid(_dot(gl.astype(BF16), wglu_ref[...]))).astype(BF16)
    o_ref[...] = _dot(perm_t_ref[...], y_tb).astype(o_ref.dtype).reshape(nb, t_steps, S5_WIDTH)


def _s5(proj3, lam_re, lam_im, log_dt, b_re, b_im, c_re, c_im, d_skip, w_glu):
    b, s, _ = proj3.shape
    g, hh = S5_GROUPS, S5_GROUP
    n = S5_NSTATE
    a_r, a_i, bb_r, bb_i = _s5_prep(lam_re, lam_im, log_dt, b_re, b_im)
    eye = jnp.eye(g, dtype=F32)
    blk_in = lambda t: jnp.einsum('hgp,gk->ghkp', t, eye).reshape(g * hh, n)
    blk_out = lambda t: jnp.einsum('ghp,gk->gpkh', t, eye).reshape(n, g * hh)
    bcat = jnp.concatenate([blk_in(bb_r), blk_in(bb_i)], axis=1).astype(BF16)
    cre = blk_out(c_re).astype(BF16)
    cim = blk_out(c_im).astype(BF16)
    ts = min(S5_TSTEPS, s)
    rows = b * ts
    r = jnp.arange(rows)
    perm = (r[:, None] % b * ts + r[:, None] // b == r[None, :]).astype(BF16)
    full = lambda shape: pl.BlockSpec(shape, lambda t: (0,) * len(shape))
    return pl.pallas_call(
        _s5_kernel,
        grid=(s // ts,),
        in_specs=[pl.BlockSpec((b, ts, S5_WIDTH), lambda t: (0, t, COL_S5_U)),
                  full((rows, rows)), full((rows, rows)),
                  full((S5_WIDTH, 2 * n)), full((n, S5_WIDTH)), full((n, S5_WIDTH)),
                  full((1, n)), full((1, n)), full((1, S5_WIDTH)), full((S5_WIDTH, S5_WIDTH))],
        out_specs=pl.BlockSpec((b, ts, S5_WIDTH), lambda t: (0, t, 0)),
        out_shape=jax.ShapeDtypeStruct((b, s, S5_WIDTH), BF16),
        scratch_shapes=[pltpu.VMEM((rows, 2 * n), F32), pltpu.VMEM((b, 2 * n), F32)],
        compiler_params=_cparams("arbitrary"),
        name="s5",
    )(proj3, perm, perm.T, bcat, cre, cim, a_r.reshape(1, n), a_i.reshape(1, n), d_skip.reshape(1, S5_WIDTH),
      w_glu.astype(BF16))


ATT_TILE = 256


def _bias_kernel(tbl_ref, o_ref):
    h = pl.program_id(0)
    t = ATT_TILE
    r = lax.broadcasted_iota(jnp.int32, (t, t), 0)
    c = lax.broadcasted_iota(jnp.int32, (t, t), 1)
    max_exact = REL_BUCKETS // 2
    far = tbl_ref[REL_BUCKETS - 1, h]
    for d in range(2):
        n = jnp.maximum(r - c + t * d, 0)
        large = max_exact + (jnp.log(jnp.maximum(n, 1).astype(F32) / max_exact)
                             / math.log(REL_MAX_DIST / max_exact) * (REL_BUCKETS - max_exact)).astype(jnp.int32)
        large = jnp.minimum(large, REL_BUCKETS - 1)
        bucket = jnp.where(n < max_exact, n, large)
        acc = jnp.zeros((t, t), F32)
        for bk in range(REL_BUCKETS):
            acc = jnp.where(bucket == bk, tbl_ref[bk, h], acc)
        o_ref[d] = (acc - far) * LOG2E


def _bias_blocks(rel_bias):
    t = ATT_TILE
    return pl.pallas_call(
        _bias_kernel,
        grid=(DIFF_HEADS,),
        in_specs=[pl.BlockSpec(memory_space=pltpu.SMEM)],
        out_specs=pl.BlockSpec((None, 2, t, t), lambda h: (h, 0, 0, 0)),
        out_shape=jax.ShapeDtypeStruct((DIFF_HEADS, 2, t, t), F32),
        compiler_params=_cparams("arbitrary"),
        name="t5_bias_blocks",
    )(rel_bias)


def _diff_kernel(q_ref, k_ref, v_ref, bias_ref, lq1_ref, lk1_ref, lq2_ref, lk2_ref, gs_ref, o_ref, s_ref, *, lam_init):
    t = ATT_TILE
    n_tiles = q_ref.shape[0] // t
    lam = (jnp.exp(jnp.sum(lq1_ref[...] * lk1_ref[...], axis=-1, keepdims=True))
           - jnp.exp(jnp.sum(lq2_ref[...] * lk2_ref[...], axis=-1, keepdims=True)) + lam_init)
    lane_q = lax.broadcasted_iota(jnp.int32, (t, LANES), 1)
    col = lax.broadcasted_iota(jnp.int32, (2 * t, t), 1)
    row = lax.broadcasted_iota(jnp.int32, (2 * t, t), 0)
    causal2 = col <= jnp.where(row < t, row, row - t)
    gs = gs_ref[...] * (1.0 - lam_init)

    for i in list(range(0, n_tiles, 2)) + list(range(n_tiles - 1 - n_tiles % 2, 0, -2)):
        q = q_ref[i * t:(i + 1) * t, :].astype(F32) * (DIFF_DQK ** -0.5 * LOG2E)
        q2 = jnp.concatenate([jnp.where(lane_q < DIFF_DQK, q, 0.0), jnp.where(lane_q >= DIFF_DQK, q, 0.0)],
                             axis=0).astype(BF16)
        m_acc = None
        for j in range(i + 1):
            s = lax.dot_general(q2, k_ref[j * t:(j + 1) * t, :], NT, preferred_element_type=F32)
            if j == i:
                b0 = bias_ref[0]
                s = jnp.where(causal2, s + jnp.concatenate([b0, b0], axis=0), -jnp.inf)
            elif j == i - 1:
                b1 = bias_ref[1]
                s = s + jnp.concatenate([b1, b1], axis=0)
            s_ref[j] = s
            mj = jnp.maximum(s[:, :LANES], s[:, LANES:])
            m_acc = mj if m_acc is None else jnp.maximum(m_acc, mj)
        m = jnp.max(m_acc, axis=-1, keepdims=True)
        l_acc = jnp.zeros((2 * t, LANES), F32)
        o2 = jnp.zeros((2 * t, DIFF_DV), F32)
        for j in range(i + 1):
            e = jnp.exp2(s_ref[j] - m)
            l_acc = l_acc + (e[:, :LANES] + e[:, LANES:])
            o2 = o2 + _dot(e.astype(BF16), v_ref[j * t:(j + 1) * t, :])
        l_sum = jnp.sum(l_acc, axis=-1, keepdims=True)
        o = o2[:t] * (1.0 / l_sum[:t]) - o2[t:] * (lam / l_sum[t:])
        o = o * lax.rsqrt(jnp.mean(o * o, axis=-1, keepdims=True) + EPS) * gs
        o_ref[i * t:(i + 1) * t, :] = o.astype(o_ref.dtype)


def _diff_attn(proj3, bias_blocks, lq1, lk1, lq2, lk2, g_subln, lam_init):
    b, s, _ = proj3.shape
    t = ATT_TILE
    tok_blk = lambda col: pl.BlockSpec((None, s, LANES), lambda i, h: (i, 0, col + h))
    vec = lambda n: pl.BlockSpec((1, n), lambda i, h: (0, 0))
    return pl.pallas_call(
        functools.partial(_diff_kernel, lam_init=lam_init),
        grid=(b, DIFF_HEADS),
        in_specs=[tok_blk(COL_DIFF_Q), tok_blk(COL_DIFF_K), tok_blk(COL_DIFF_V),
                  pl.BlockSpec((None, 2, t, t), lambda i, h: (h, 0, 0, 0)),
                  vec(DIFF_DQK), vec(DIFF_DQK), vec(DIFF_DQK), vec(DIFF_DQK), vec(DIFF_DV)],
        out_specs=pl.BlockSpec((None, s, LANES), lambda i, h: (i, 0, h)),
        out_shape=jax.ShapeDtypeStruct((b, s, DIFF_HEADS * DIFF_DV), BF16),
        scratch_shapes=[pltpu.VMEM((s // t, 2 * t, t), F32)],
        compiler_params=_cparams("parallel", "parallel"),
        name="diff_attn",
    )(proj3, proj3, proj3, bias_blocks,
      lq1.reshape(1, -1), lk1.reshape(1, -1), lq2.reshape(1, -1), lk2.reshape(1, -1), g_subln.reshape(1, -1))


def _outproj_kernel(yr_ref, ys_ref, yd_ref, x_ref, w_ref, g_ref, o_ref):
    n_r = yr_ref.shape[1]
    n_s = ys_ref.shape[1]
    acc = _dot(yr_ref[...], w_ref[:n_r, :])
    acc = acc + _dot(ys_ref[...], w_ref[n_r:n_r + n_s, :])
    acc = acc + _dot(yd_ref[...], w_ref[n_r + n_s:, :])
    o_ref[...] = x_ref[...] + _rms(acc, g_ref[...])


def _outproj(y_ret, y_s5, y_diff, x2, w, g, l):
    tok, d = x2.shape
    tm = min(512, tok)
    row = lambda n: pl.BlockSpec((tm, n), lambda i: (i, 0))
    return pl.pallas_call(
        _outproj_kernel,
        grid=(tok // tm,),
        in_specs=[row(y_ret.shape[1]), row(y_s5.shape[1]), row(y_diff.shape[1]), row(d),
                  pl.BlockSpec((None, d, d), lambda i: (l, 0, 0)), pl.BlockSpec((1, d), lambda i: (0, 0))],
        out_specs=row(d),
        out_shape=jax.ShapeDtypeStruct((tok, d), F32),
        compiler_params=_cparams("parallel"),
        name="outproj",
    )(y_ret, y_s5, y_diff, x2, w, g)


def _memkv_kernel(m_ref, g_ref, w_ref, o_ref):
    o_ref[...] = _dot(_rms(m_ref[...], g_ref[...]).astype(BF16), w_ref[...]).astype(o_ref.dtype)


def _memkv(mem, g, w, l):
    b, m, d = mem.shape
    n = w.shape[2]
    return pl.pallas_call(
        _memkv_kernel,
        grid=(b,),
        in_specs=[pl.BlockSpec((None, m, d), lambda i: (i, 0, 0)),
                  pl.BlockSpec((1, d), lambda i: (0, 0)), pl.BlockSpec((None, d, n), lambda i: (l, 0, 0))],
        out_specs=pl.BlockSpec((None, m, n), lambda i: (i, 0, 0)),
        out_shape=jax.ShapeDtypeStruct((b, m, n), BF16),
        compiler_params=_cparams("parallel"),
        name="mem_kv",
    )(mem, g, w)


def _xattn_kernel(x_ref, kv_ref, wq_ref, wo_ref, gpre_ref, gpost_ref, o_ref):
    x = x_ref[...]
    hn = _rms(x, gpre_ref[...]).astype(BF16)
    q = (_dot(hn, wq_ref[...]) * (X_HEAD_DIM ** -0.5)).astype(BF16)
    outs = []
    for hd in range(X_HEADS):
        lo = hd * X_HEAD_DIM
        kh = kv_ref[:, lo:lo + X_HEAD_DIM]
        vh = kv_ref[:, X_WIDTH + lo:X_WIDTH + lo + X_HEAD_DIM]
        s = lax.dot_general(q[:, lo:lo + X_HEAD_DIM], kh, NT, preferred_element_type=F32)
        e = jnp.exp(s - jnp.max(s, axis=-1, keepdims=True))
        l_sum = jnp.sum(e, axis=-1, keepdims=True)
        outs.append(_dot(e.astype(BF16), vh) / l_sum)
    o = jnp.concatenate(outs, axis=-1).astype(BF16)
    o_ref[...] = x + _rms(_dot(o, wo_ref[...]), gpost_ref[...])


def _xattn(x3, kv, wq, wo, gpre, gpost, l):
    b, s, d = x3.shape
    m = kv.shape[1]
    tq = min(1024, s)
    full = lambda shape: pl.BlockSpec(shape, lambda i, t: (0,) * len(shape))
    return pl.pallas_call(
        _xattn_kernel,
        grid=(b, s // tq),
        in_specs=[pl.BlockSpec((None, tq, d), lambda i, t: (i, t, 0)),
                  pl.BlockSpec((None, m, 2 * X_WIDTH), lambda i, t: (i, 0, 0)),
                  pl.BlockSpec((None, d, X_WIDTH), lambda i, t: (l, 0, 0)),
                  pl.BlockSpec((None, X_WIDTH, d), lambda i, t: (l, 0, 0)), full((1, d)), full((1, d))],
        out_specs=pl.BlockSpec((None, tq, d), lambda i, t: (i, t, 0)),
        out_shape=jax.ShapeDtypeStruct((b, s, d), F32),
        compiler_params=_cparams("parallel", "parallel"),
        name="xattn",
    )(x3, kv, wq, wo, gpre, gpost)


def _mlp_kernel(x_ref, wu_ref, wd_ref, gpre_ref, gpost_ref, o_ref, h_ref, acc_ref):
    j = pl.program_id(1)

    @pl.when(j == 0)
    def _():
        h_ref[...] = _rms(x_ref[...], gpre_ref[...]).astype(BF16)
        acc_ref[...] = jnp.zeros_like(acc_ref)

    u = jnp.maximum(_dot(h_ref[...], wu_ref[...]), 0.0)
    acc_ref[...] += _dot((u * u).astype(BF16), wd_ref[...])

    @pl.when(j == pl.num_programs(1) - 1)
    def _():
        o_ref[...] = x_ref[...] + _rms(acc_ref[...], gpost_ref[...])


def _mlp(x2, wu, wd, gpre, gpost, l):
    tok, d = x2.shape
    ff = wu.shape[2]
    tm = min(512, tok)
    tf = 1024
    return pl.pallas_call(
        _mlp_kernel,
        grid=(tok // tm, ff // tf),
        in_specs=[pl.BlockSpec((tm, d), lambda i, j: (i, 0)),
                  pl.BlockSpec((None, d, tf), lambda i, j: (l, 0, j)),
                  pl.BlockSpec((None, tf, d), lambda i, j: (l, j, 0)),
                  pl.BlockSpec((1, d), lambda i, j: (0, 0)), pl.BlockSpec((1, d), lambda i, j: (0, 0))],
        out_specs=pl.BlockSpec((tm, d), lambda i, j: (i, 0)),
        out_shape=jax.ShapeDtypeStruct((tok, d), F32),
        scratch_shapes=[pltpu.VMEM((tm, d), BF16), pltpu.VMEM((tm, d), F32)],
        compiler_params=_cparams("parallel", "arbitrary"),
        name="mlp",
    )(x2, wu, wd, gpre, gpost)


def kernel(x, mem, positions, rel_bias, w_in, w_out, lam_re, lam_im, log_dt, b_re, b_im, c_re, c_im, d_skip, w_glu, lam_q1, lam_k1, lam_q2, lam_k2, g_subln, w_xq, w_xkv, w_xo, w_up, w_down, g_mix_pre, g_mix_post, g_mem, g_x_pre, g_x_post, g_mlp_pre, g_mlp_post):
    b, s, d = x.shape
    tok = b * s
    row = lambda g: g.reshape(1, -1)
    cos2, sin2 = _rot_tables(positions)
    bias_blocks = _bias_blocks(rel_bias)
    w_in, w_out, w_xq, w_xkv, w_xo, w_up, w_down = (
        w.astype(BF16) for w in (w_in, w_out, w_xq, w_xkv, w_xo, w_up, w_down))
    x2 = x.reshape(tok, d)
    for l in range(DEPTH):
        lam_init = 0.8 - 0.6 * math.exp(-0.3 * l)
        proj3 = _inproj(x2, row(g_mix_pre[l]), w_in, l).reshape(b, s, IN_WIDTH)
        y_ret = _retention(proj3, cos2, sin2)
        y_s5 = _s5(proj3, lam_re[l], lam_im[l], log_dt[l], b_re[l], b_im[l], c_re[l], c_im[l], d_skip[l], w_glu[l])
        y_diff = _diff_attn(proj3, bias_blocks, lam_q1[l], lam_k1[l], lam_q2[l], lam_k2[l], g_subln[l], lam_init)
        x2 = _outproj(y_ret.reshape(tok, -1), y_s5.reshape(tok, -1), y_diff.reshape(tok, -1), x2,
                      w_out, row(g_mix_post[l]), l)
        kv = _memkv(mem, row(g_mem[l]), w_xkv, l)
        x2 = _xattn(x2.reshape(b, s, d), kv, w_xq, w_xo, row(g_x_pre[l]), row(g_x_post[l]), l).reshape(tok, d)
        x2 = _mlp(x2, w_up, w_down, row(g_mlp_pre[l]), row(g_mlp_post[l]), l)
    return x2.reshape(b, s, d)
```

```python
import functools
import math

import jax
import jax.numpy as jnp
from jax import lax
from jax.experimental import pallas as pl
from jax.experimental.pallas import tpu as pltpu

F32 = jnp.float32
BF16 = jnp.bfloat16

DEPTH = 2
RET_HEADS = 4
RET_DK = 128
RET_CHUNK = 128
RET_WIDTH = RET_HEADS * RET_DK
S5_WIDTH = 512
S5_GROUP = 16
S5_GROUPS = 32
S5_STATE = 64
S5_NSTATE = S5_GROUPS * S5_STATE
DIFF_HEADS = 8
DIFF_DV = 128
DIFF_DQK = 64
REL_BUCKETS = 32
REL_MAX_DIST = 128
X_HEADS = 4
X_HEAD_DIM = 128
X_WIDTH = X_HEADS * X_HEAD_DIM
EPS = 1e-6
LOG2E = math.log2(math.e)
IN_WIDTH = 5632
COL_RET_Q, COL_RET_K, COL_RET_V, COL_RET_G, COL_S5_U = 0, 1, 2, 3, 4
COL_DIFF_Q, COL_DIFF_K, COL_DIFF_V = 20, 28, 36

LANES = 128
SUBLANES = 8
MXU_TILE = 256

VMEM_LIMIT = 56 * 1024 * 1024
INPROJ_TM = 512
OUTPROJ_TM = 512
XATTN_TQ = 1024
MLP_TM = 512
MLP_TF = 1024
RET_TBLK = 512
S5_SCAN_LANES = 512
S5_TSTEPS = 32
ATT_TILE = MXU_TILE

NT = (((1,), (1,)), ((), ()))
TN = (((0,), (0,)), ((), ()))


def _cparams(*sem):
    return pltpu.CompilerParams(dimension_semantics=sem, vmem_limit_bytes=VMEM_LIMIT)


def _rms(x, g):
    ms = jnp.mean(x * x, axis=-1, keepdims=True)
    return x * lax.rsqrt(ms + EPS) * g


def _dot(a, b):
    return jnp.dot(a, b, preferred_element_type=F32)


def _inproj_kernel(x_ref, g_ref, w_ref, o_ref):
    o_ref[...] = _dot(_rms(x_ref[...], g_ref[...]).astype(BF16), w_ref[...]).astype(o_ref.dtype)


def _inproj(x2, g, w, l):
    tok, d = x2.shape
    n = w.shape[2]
    tm = min(INPROJ_TM, tok)
    return pl.pallas_call(
        _inproj_kernel,
        grid=(tok // tm,),
        in_specs=[
            pl.BlockSpec((tm, d), lambda i: (i, 0)),
            pl.BlockSpec((1, d), lambda i: (0, 0)),
            pl.BlockSpec((None, d, n), lambda i: (l, 0, 0), pipeline_mode=pl.Buffered(1)),
        ],
        out_specs=pl.BlockSpec((tm, n), lambda i: (i, 0)),
        out_shape=jax.ShapeDtypeStruct((tok, n), BF16),
        compiler_params=_cparams("parallel"),
        name="inproj",
    )(x2, g, w)


def _rot_kernel(pos_ref, cos_ref, sin_ref):
    hs = pos_ref.shape[0] // 2
    half = RET_DK // 2
    lane = lax.broadcasted_iota(jnp.int32, (1, LANES), 1)
    lo = lane < half
    j = jnp.where(lo, lane, lane - half).astype(F32)
    inv = jnp.exp(j * (-math.log(10000.0) / half))
    pos = pos_ref[...].astype(F32)
    ang = jnp.where(lo, pos[:hs], pos[hs:]) * inv
    c = jnp.cos(ang)
    s = jnp.sin(ang)
    c_sw = pltpu.roll(c, half, 1)
    s_sw = pltpu.roll(s, half, 1)
    cos_ref[:hs] = jnp.where(lo, c, c_sw)
    cos_ref[hs:] = jnp.where(lo, c_sw, c)
    sin_ref[:hs] = jnp.where(lo, -s, s_sw)
    sin_ref[hs:] = jnp.where(lo, -s_sw, s)


def _rot_tables(positions):
    b, s = positions.shape
    out = jax.ShapeDtypeStruct((b, s, LANES), F32)
    return pl.pallas_call(
        _rot_kernel,
        grid=(b,),
        in_specs=[pl.BlockSpec((None, s, 1), lambda i: (i, 0, 0))],
        out_specs=[pl.BlockSpec((None, s, LANES), lambda i: (i, 0, 0))] * 2,
        out_shape=[out, out],
        compiler_params=_cparams("parallel"),
        name="rot_tables",
    )(positions.reshape(b, s, 1))


def _ret_kernel(q_ref, k_ref, v_ref, g_ref, cos_ref, sin_ref, dec_ref, xi_ref, zeta_ref, gc_ref, o_ref, r_ref):
    c_len = RET_CHUNK
    scale = RET_DK ** -0.5

    @pl.when(pl.program_id(1) == 0)
    def _():
        r_ref[...] = jnp.zeros_like(r_ref)

    n_c = q_ref.shape[0] // c_len
    stage = []
    for c in range(n_c):
        rows = slice(c * c_len, (c + 1) * c_len)
        cs = cos_ref[rows, :]
        sn = sin_ref[rows, :]
        for h in range(RET_HEADS):
            cols = slice(h * RET_DK, (h + 1) * RET_DK)
            q = q_ref[rows, cols].astype(F32)
            k = k_ref[rows, cols].astype(F32)
            qr = q * cs + pltpu.roll(q, RET_DK // 2, 1) * sn
            kr = (k * cs + pltpu.roll(k, RET_DK // 2, 1) * sn) * scale
            vb = v_ref[rows, cols]
            inner = lax.dot_general(qr.astype(BF16), kr.astype(BF16), NT, preferred_element_type=F32) * dec_ref[h]
            d_r = lax.dot_general((kr * zeta_ref[h]).astype(BF16), vb, TN, preferred_element_type=F32)
            lhs = jnp.concatenate([inner.astype(BF16), (qr * xi_ref[h]).astype(BF16)], axis=1)
            stage.append((lhs, vb, d_r))
    r_states = [r_ref[h] for h in range(RET_HEADS)]
    for c in range(n_c):
        rows = slice(c * c_len, (c + 1) * c_len)
        for h in range(RET_HEADS):
            cols = slice(h * RET_DK, (h + 1) * RET_DK)
            lhs, vb, d_r = stage[c * RET_HEADS + h]
            o = _dot(lhs, jnp.concatenate([vb, r_states[h].astype(BF16)], axis=0))
            r_states[h] = gc_ref[h] * r_states[h] + d_r
            o = o * lax.rsqrt(jnp.mean(o * o, axis=-1, keepdims=True) + EPS)
            gate = g_ref[rows, cols].astype(F32)
            o_ref[rows, cols] = (o * (gate * jax.nn.sigmoid(gate))).astype(o_ref.dtype)
    for h in range(RET_HEADS):
        r_ref[h] = r_states[h]


def _ret_consts():
    h, c = RET_HEADS, RET_CHUNK
    log_g = jnp.log(1.0 - 2.0 ** (-5.0 - jnp.arange(h, dtype=F32)))
    idx = jnp.arange(c, dtype=F32)
    dist = idx[:, None] - idx[None, :]
    decay = jnp.where(dist >= 0, jnp.exp(jnp.maximum(dist, 0.0)[None] * log_g[:, None, None]), 0.0)
    xi = jnp.exp((idx + 1.0)[None, :] * log_g[:, None])
    zeta = jnp.exp((c - 1.0 - idx)[None, :] * log_g[:, None])
    g_chunk = jnp.exp(c * log_g)
    bc = lambda t: jnp.broadcast_to(t[:, :, None], (h, c, LANES))
    gcb = jnp.broadcast_to(g_chunk[:, None, None], (h, c, LANES))
    return decay, bc(xi), bc(zeta), gcb


def _retention(proj3, cos2, sin2):
    b, s, _ = proj3.shape
    tb = min(RET_TBLK, s)
    decay, xi, zeta, gcb = _ret_consts()
    tok_blk = lambda col: pl.BlockSpec((None, tb, RET_WIDTH), lambda i, t: (i, t, col))
    pos_blk = pl.BlockSpec((None, tb, LANES), lambda i, t: (i, t, 0))
    const = pl.BlockSpec((RET_HEADS, RET_CHUNK, LANES), lambda i, t: (0, 0, 0))
    return pl.pallas_call(
        _ret_kernel,
        grid=(b, s // tb),
        in_specs=[tok_blk(COL_RET_Q), tok_blk(COL_RET_K), tok_blk(COL_RET_V), tok_blk(COL_RET_G),
                  pos_blk, pos_blk, const, const, const, const],
        out_specs=pl.BlockSpec((None, tb, RET_WIDTH), lambda i, t: (i, t, 0)),
        out_shape=jax.ShapeDtypeStruct((b, s, RET_WIDTH), BF16),
        scratch_shapes=[pltpu.VMEM((RET_HEADS, RET_DK, RET_DK), F32)],
        compiler_params=_cparams("parallel", "arbitrary"),
        name="retention",
    )(proj3, proj3, proj3, proj3, cos2, sin2, decay, xi, zeta, gcb)


def _s5prep_kernel(lr_ref, li_ref, ldt_ref, br_ref, bi_ref, ar_ref, ai_ref, bbr_ref, bbi_ref):
    lr = jnp.minimum(lr_ref[...], -1e-4)
    li = li_ref[...]
    dt = jnp.exp(ldt_ref[...])
    mag = jnp.exp(lr * dt)
    ar = mag * jnp.cos(li * dt)
    ai = mag * jnp.sin(li * dt)
    nr, ni = ar - 1.0, ai
    den = lr * lr + li * li
    fr = (nr * lr + ni * li) / den
    fi = (ni * lr - nr * li) / den
    br = br_ref[...]
    bi = bi_ref[...]
    bbr_ref[...] = fr[None] * br - fi[None] * bi
    bbi_ref[...] = fr[None] * bi + fi[None] * br
    ar_ref[...] = ar
    ai_ref[...] = ai


def _s5_prep(lam_re, lam_im, log_dt, b_re, b_im):
    g, p = lam_re.shape
    ab = jax.ShapeDtypeStruct((g, p), F32)
    bb = jax.ShapeDtypeStruct((S5_GROUP, g, p), F32)
    return pl.pallas_call(
        _s5prep_kernel,
        out_shape=[ab, ab, bb, bb],
        name="s5_prep",
    )(lam_re, lam_im, log_dt.reshape(g, 1), b_re.transpose(2, 0, 1), b_im.transpose(2, 0, 1))


def _s5_kernel(u_ref, perm_ref, perm_t_ref, bcat_ref, cre_ref, cim_ref, ar_ref, ai_ref, dskip_ref, wglu_ref, o_ref,
               bu_ref, st_ref):
    n = S5_NSTATE
    nb, t_steps, _ = u_ref.shape

    @pl.when(pl.program_id(0) == 0)
    def _():
        st_ref[...] = jnp.zeros_like(st_ref)

    u = _dot(perm_ref[...], u_ref[...].reshape(nb * t_steps, S5_WIDTH)).astype(BF16)
    hw = S5_WIDTH // 2
    hn = n // 2
    for part in range(2):
        for half in range(2):
            cols = slice(part * n + half * hn, part * n + (half + 1) * hn)
            bu_ref[:, cols] = _dot(u[:, half * hw:(half + 1) * hw], bcat_ref[half * hw:(half + 1) * hw, cols])

    for lo in range(0, n, S5_SCAN_LANES):
        sl_r = slice(lo, lo + S5_SCAN_LANES)
        sl_i = slice(n + lo, n + lo + S5_SCAN_LANES)
        a_r = jnp.broadcast_to(ar_ref[:, sl_r], (nb, S5_SCAN_LANES))
        a_i = jnp.broadcast_to(ai_ref[:, sl_r], (nb, S5_SCAN_LANES))

        xr = st_ref[:, sl_r]
        xi = st_ref[:, sl_i]
        for t in range(t_steps):
            rs = slice(t * nb, (t + 1) * nb)
            xr, xi = (a_r * xr - a_i * xi + bu_ref[rs, sl_r],
                      a_r * xi + a_i * xr + bu_ref[rs, sl_i])
            bu_ref[rs, sl_r] = xr
            bu_ref[rs, sl_i] = xi
        st_ref[:, sl_r] = xr
        st_ref[:, sl_i] = xi

    ys = []
    for half in range(2):
        xr = bu_ref[:, half * hn:(half + 1) * hn].astype(BF16)
        xi = bu_ref[:, n + half * hn:n + (half + 1) * hn].astype(BF16)
        rows = slice(half * hn, (half + 1) * hn)
        cols = slice(half * hw, (half + 1) * hw)
        ys.append(_dot(xr, cre_ref[rows, cols]) - _dot(xi, cim_ref[rows, cols]))
    y = jnp.concatenate(ys, axis=1) + dskip_ref[...] * u.astype(F32)
    gl = jax.nn.gelu(y)
    y_tb = (gl * jax.nn.sigmoid(_dot(gl.astype(BF16), wglu_ref[...]))).astype(BF16)
    o_ref[...] = _dot(perm_t_ref[...], y_tb).astype(o_ref.dtype).reshape(nb, t_steps, S5_WIDTH)


def _s5(proj3, lam_re, lam_im, log_dt, b_re, b_im, c_re, c_im, d_skip, w_glu):
    b, s, _ = proj3.shape
    g, hh = S5_GROUPS, S5_GROUP
    n = S5_NSTATE
    a_r, a_i, bb_r, bb_i = _s5_prep(lam_re, lam_im, log_dt, b_re, b_im)
    eye = jnp.eye(g, dtype=F32)
    blk_in = lambda t: jnp.einsum('hgp,gk->ghkp', t, eye).reshape(g * hh, n)
    blk_out = lambda t: jnp.einsum('ghp,gk->gpkh', t, eye).reshape(n, g * hh)
    bcat = jnp.concatenate([blk_in(bb_r), blk_in(bb_i)], axis=1).astype(BF16)
    cre = blk_out(c_re).astype(BF16)
    cim = blk_out(c_im).astype(BF16)
    ts = min(S5_TSTEPS, s)
    rows = b * ts
    r = jnp.arange(rows)
    perm = (r[:, None] % b * ts + r[:, None] // b == r[None, :]).astype(BF16)
    full = lambda shape: pl.BlockSpec(shape, lambda t: (0,) * len(shape))
    return pl.pallas_call(
        _s5_kernel,
        grid=(s // ts,),
        in_specs=[pl.BlockSpec((b, ts, S5_WIDTH), lambda t: (0, t, COL_S5_U)),
                  full((rows, rows)), full((rows, rows)),
                  full((S5_WIDTH, 2 * n)), full((n, S5_WIDTH)), full((n, S5_WIDTH)),
                  full((1, n)), full((1, n)), full((1, S5_WIDTH)), full((S5_WIDTH, S5_WIDTH))],
        out_specs=pl.BlockSpec((b, ts, S5_WIDTH), lambda t: (0, t, 0)),
        out_shape=jax.ShapeDtypeStruct((b, s, S5_WIDTH), BF16),
        scratch_shapes=[pltpu.VMEM((rows, 2 * n), F32), pltpu.VMEM((b, 2 * n), F32)],
        compiler_params=_cparams("arbitrary"),
        name="s5",
    )(proj3, perm, perm.T, bcat, cre, cim, a_r.reshape(1, n), a_i.reshape(1, n), d_skip.reshape(1, S5_WIDTH),
      w_glu.astype(BF16))


def _bias_kernel(tbl_ref, o_ref):
    h = pl.program_id(0)
    t = ATT_TILE
    r = lax.broadcasted_iota(jnp.int32, (t, t), 0)
    c = lax.broadcasted_iota(jnp.int32, (t, t), 1)
    max_exact = REL_BUCKETS // 2
    far = tbl_ref[REL_BUCKETS - 1, h]
    for d in range(2):
        n = jnp.maximum(r - c + t * d, 0)
        large = max_exact + (jnp.log(jnp.maximum(n, 1).astype(F32) / max_exact)
                             / math.log(REL_MAX_DIST / max_exact) * (REL_BUCKETS - max_exact)).astype(jnp.int32)
        large = jnp.minimum(large, REL_BUCKETS - 1)
        bucket = jnp.where(n < max_exact, n, large)
        acc = jnp.zeros((t, t), F32)
        for bk in range(REL_BUCKETS):
            acc = jnp.where(bucket == bk, tbl_ref[bk, h], acc)
        o_ref[d] = (acc - far) * LOG2E


def _bias_blocks(rel_bias):
    t = ATT_TILE
    return pl.pallas_call(
        _bias_kernel,
        grid=(DIFF_HEADS,),
        in_specs=[pl.BlockSpec(memory_space=pltpu.SMEM)],
        out_specs=pl.BlockSpec((None, 2, t, t), lambda h: (h, 0, 0, 0)),
        out_shape=jax.ShapeDtypeStruct((DIFF_HEADS, 2, t, t), F32),
        compiler_params=_cparams("arbitrary"),
        name="t5_bias_blocks",
    )(rel_bias)


def _diff_kernel(q_ref, k_ref, v_ref, bias_ref, lq1_ref, lk1_ref, lq2_ref, lk2_ref, gs_ref, o_ref, s_ref, *, lam_init):
    t = ATT_TILE
    n_tiles = q_ref.shape[0] // t
    lam = (jnp.exp(jnp.sum(lq1_ref[...] * lk1_ref[...], axis=-1, keepdims=True))
           - jnp.exp(jnp.sum(lq2_ref[...] * lk2_ref[...], axis=-1, keepdims=True)) + lam_init)
    lane_q = lax.broadcasted_iota(jnp.int32, (t, LANES), 1)
    col = lax.broadcasted_iota(jnp.int32, (2 * t, t), 1)
    row = lax.broadcasted_iota(jnp.int32, (2 * t, t), 0)
    causal2 = col <= jnp.where(row < t, row, row - t)
    gs = gs_ref[...] * (1.0 - lam_init)

    for i in list(range(1, n_tiles, 2)) + list(range((n_tiles - 1) // 2 * 2, -1, -2)):
        q = q_ref[i * t:(i + 1) * t, :].astype(F32) * (DIFF_DQK ** -0.5 * LOG2E)
        q2 = jnp.concatenate([jnp.where(lane_q < DIFF_DQK, q, 0.0), jnp.where(lane_q >= DIFF_DQK, q, 0.0)],
                             axis=0).astype(BF16)
        m_acc = None
        for j in range(i + 1):
            s = lax.dot_general(q2, k_ref[j * t:(j + 1) * t, :], NT, preferred_element_type=F32)
            if j == i:
                b0 = bias_ref[0]
                s = jnp.where(causal2, s + jnp.concatenate([b0, b0], axis=0), -jnp.inf)
            elif j == i - 1:
                b1 = bias_ref[1]
                s = s + jnp.concatenate([b1, b1], axis=0)
            s_ref[j] = s
            mj = jnp.maximum(s[:, :LANES], s[:, LANES:])
            m_acc = mj if m_acc is None else jnp.maximum(m_acc, mj)
        m = jnp.max(m_acc, axis=-1, keepdims=True)
        l_acc = jnp.zeros((2 * t, LANES), F32)
        o2 = jnp.zeros((2 * t, DIFF_DV), F32)
        for j in range(i + 1):
            e = jnp.exp2(s_ref[j] - m)
            l_acc = l_acc + (e[:, :LANES] + e[:, LANES:])
            o2 = o2 + _dot(e.astype(BF16), v_ref[j * t:(j + 1) * t, :])
        l_sum = jnp.sum(l_acc, axis=-1, keepdims=True)
        o = o2[:t] * (1.0 / l_sum[:t]) - o2[t:] * (lam / l_sum[t:])
        o = o * lax.rsqrt(jnp.mean(o * o, axis=-1, keepdims=True) + EPS) * gs
        o_ref[i * t:(i + 1) * t, :] = o.astype(o_ref.dtype)


def _diff_attn(proj3, bias_blocks, lq1, lk1, lq2, lk2, g_subln, lam_init):
    b, s, _ = proj3.shape
    t = ATT_TILE
    tok_blk = lambda col: pl.BlockSpec((None, s, LANES), lambda i, h: (i, 0, col + h))
    vec = lambda n: pl.BlockSpec((1, n), lambda i, h: (0, 0))
    return pl.pallas_call(
        functools.partial(_diff_kernel, lam_init=lam_init),
        grid=(b, DIFF_HEADS),
        in_specs=[tok_blk(COL_DIFF_Q), tok_blk(COL_DIFF_K), tok_blk(COL_DIFF_V),
                  pl.BlockSpec((None, 2, t, t), lambda i, h: (h, 0, 0, 0)),
                  vec(DIFF_DQK), vec(DIFF_DQK), vec(DIFF_DQK), vec(DIFF_DQK), vec(DIFF_DV)],
        out_specs=pl.BlockSpec((None, s, LANES), lambda i, h: (i, 0, h)),
        out_shape=jax.ShapeDtypeStruct((b, s, DIFF_HEADS * DIFF_DV), BF16),
        scratch_shapes=[pltpu.VMEM((s // t, 2 * t, t), F32)],
        compiler_params=_cparams("parallel", "parallel"),
        name="diff_attn",
    )(proj3, proj3, proj3, bias_blocks,
      lq1.reshape(1, -1), lk1.reshape(1, -1), lq2.reshape(1, -1), lk2.reshape(1, -1), g_subln.reshape(1, -1))


def _outproj_kernel(yr_ref, ys_ref, yd_ref, x_ref, w_ref, g_ref, o_ref):
    n_r = yr_ref.shape[1]
    n_s = ys_ref.shape[1]
    acc = _dot(yr_ref[...], w_ref[:n_r, :])
    acc = acc + _dot(ys_ref[...], w_ref[n_r:n_r + n_s, :])
    acc = acc + _dot(yd_ref[...], w_ref[n_r + n_s:, :])
    o_ref[...] = x_ref[...] + _rms(acc, g_ref[...])


def _outproj(y_ret, y_s5, y_diff, x2, w, g, l):
    tok, d = x2.shape
    tm = min(OUTPROJ_TM, tok)
    row = lambda n: pl.BlockSpec((tm, n), lambda i: (i, 0))
    return pl.pallas_call(
        _outproj_kernel,
        grid=(tok // tm,),
        in_specs=[row(y_ret.shape[1]), row(y_s5.shape[1]), row(y_diff.shape[1]), row(d),
                  pl.BlockSpec((None, d, d), lambda i: (l, 0, 0)), pl.BlockSpec((1, d), lambda i: (0, 0))],
        out_specs=row(d),
        out_shape=jax.ShapeDtypeStruct((tok, d), F32),
        compiler_params=_cparams("parallel"),
        name="outproj",
    )(y_ret, y_s5, y_diff, x2, w, g)


def _memkv_kernel(m_ref, g_ref, w_ref, o_ref):
    o_ref[...] = _dot(_rms(m_ref[...], g_ref[...]).astype(BF16), w_ref[...]).astype(o_ref.dtype)


def _memkv(mem, g, w, l):
    b, m, d = mem.shape
    n = w.shape[2]
    return pl.pallas_call(
        _memkv_kernel,
        grid=(b,),
        in_specs=[pl.BlockSpec((None, m, d), lambda i: (i, 0, 0)),
                  pl.BlockSpec((1, d), lambda i: (0, 0)), pl.BlockSpec((None, d, n), lambda i: (l, 0, 0))],
        out_specs=pl.BlockSpec((None, m, n), lambda i: (i, 0, 0)),
        out_shape=jax.ShapeDtypeStruct((b, m, n), BF16),
        compiler_params=_cparams("parallel"),
        name="mem_kv",
    )(mem, g, w)


def _xattn_kernel(x_ref, kv_ref, wq_ref, wo_ref, gpre_ref, gpost_ref, o_ref):
    x = x_ref[...]
    hn = _rms(x, gpre_ref[...]).astype(BF16)
    q = (_dot(hn, wq_ref[...]) * (X_HEAD_DIM ** -0.5)).astype(BF16)
    outs = []
    for hd in range(X_HEADS):
        lo = hd * X_HEAD_DIM
        kh = kv_ref[:, lo:lo + X_HEAD_DIM]
        vh = kv_ref[:, X_WIDTH + lo:X_WIDTH + lo + X_HEAD_DIM]
        s = lax.dot_general(q[:, lo:lo + X_HEAD_DIM], kh, NT, preferred_element_type=F32)
        e = jnp.exp(s - jnp.max(s, axis=-1, keepdims=True))
        l_sum = jnp.sum(e, axis=-1, keepdims=True)
        outs.append(_dot(e.astype(BF16), vh) / l_sum)
    o = jnp.concatenate(outs, axis=-1).astype(BF16)
    o_ref[...] = x + _rms(_dot(o, wo_ref[...]), gpost_ref[...])


def _xattn(x3, kv, wq, wo, gpre, gpost, l):
    b, s, d = x3.shape
    m = kv.shape[1]
    tq = min(XATTN_TQ, s)
    full = lambda shape: pl.BlockSpec(shape, lambda i, t: (0,) * len(shape))
    return pl.pallas_call(
        _xattn_kernel,
        grid=(b, s // tq),
        in_specs=[pl.BlockSpec((None, tq, d), lambda i, t: (i, t, 0)),
                  pl.BlockSpec((None, m, 2 * X_WIDTH), lambda i, t: (i, 0, 0)),
                  pl.BlockSpec((None, d, X_WIDTH), lambda i, t: (l, 0, 0)),
                  pl.BlockSpec((None, X_WIDTH, d), lambda i, t: (l, 0, 0)), full((1, d)), full((1, d))],
        out_specs=pl.BlockSpec((None, tq, d), lambda i, t: (i, t, 0)),
        out_shape=jax.ShapeDtypeStruct((b, s, d), F32),
        compiler_params=_cparams("parallel", "parallel"),
        name="xattn",
    )(x3, kv, wq, wo, gpre, gpost)


def _mlp_kernel(x_ref, wu_ref, wd_ref, gpre_ref, gpost_ref, o_ref, h_ref, acc_ref):
    j = pl.program_id(1)
    last = pl.num_programs(1) - 1

    def partial_out(h):
        u = jnp.maximum(_dot(h, wu_ref[...]), 0.0)
        return _dot((u * u).astype(BF16), wd_ref[...])

    @pl.when(j == 0)
    def _():
        h = _rms(x_ref[...], gpre_ref[...]).astype(BF16)
        h_ref[...] = h
        acc_ref[...] = partial_out(h)

    @pl.when((j > 0) & (j < last))
    def _():
        acc_ref[...] += partial_out(h_ref[...])

    @pl.when(j == last)
    def _():
        o_ref[...] = x_ref[...] + _rms(acc_ref[...] + partial_out(h_ref[...]), gpost_ref[...])


def _mlp(x2, wu, wd, gpre, gpost, l):
    tok, d = x2.shape
    ff = wu.shape[2]
    tm = min(MLP_TM, tok)
    tf = MLP_TF
    return pl.pallas_call(
        _mlp_kernel,
        grid=(tok // tm, ff // tf),
        in_specs=[pl.BlockSpec((tm, d), lambda i, j: (i, 0)),
                  pl.BlockSpec((None, d, tf), lambda i, j: (l, 0, j)),
                  pl.BlockSpec((None, tf, d), lambda i, j: (l, j, 0)),
                  pl.BlockSpec((1, d), lambda i, j: (0, 0)), pl.BlockSpec((1, d), lambda i, j: (0, 0))],
        out_specs=pl.BlockSpec((tm, d), lambda i, j: (i, 0)),
        out_shape=jax.ShapeDtypeStruct((tok, d), F32),
        scratch_shapes=[pltpu.VMEM((tm, d), BF16), pltpu.VMEM((tm, d), F32)],
        compiler_params=_cparams("parallel", "arbitrary"),
        name="mlp",
    )(x2, wu, wd, gpre, gpost)


def kernel(x, mem, positions, rel_bias, w_in, w_out, lam_re, lam_im, log_dt, b_re, b_im, c_re, c_im, d_skip, w_glu, lam_q1, lam_k1, lam_q2, lam_k2, g_subln, w_xq, w_xkv, w_xo, w_up, w_down, g_mix_pre, g_mix_post, g_mem, g_x_pre, g_x_post, g_mlp_pre, g_mlp_post):
    b, s, d = x.shape
    tok = b * s
    row = lambda g: g.reshape(1, -1)
    cos2, sin2 = _rot_tables(positions)
    bias_blocks = _bias_blocks(rel_bias)
    w_in, w_out, w_xq, w_xkv, w_xo, w_up, w_down = (
        w.astype(BF16) for w in (w_in, w_out, w_xq, w_xkv, w_xo, w_up, w_down))
    x2 = x.reshape(tok, d)
    for l in range(DEPTH):
        lam_init = 0.8 - 0.6 * math.exp(-0.3 * l)
        proj3 = _inproj(x2, row(g_mix_pre[l]), w_in, l).reshape(b, s, IN_WIDTH)
        y_ret = _retention(proj3, cos2, sin2)
        y_s5 = _s5(proj3, lam_re[l], lam_im[l], log_dt[l], b_re[l], b_im[l], c_re[l], c_im[l], d_skip[l], w_glu[l])
        y_diff = _diff_attn(proj3, bias_blocks, lam_q1[l], lam_k1[l], lam_q2[l], lam_k2[l], g_subln[l], lam_init)
        x2 = _outproj(y_ret.reshape(tok, -1), y_s5.reshape(tok, -1), y_diff.reshape(tok, -1), x2,
                      w_out, row(g_mix_post[l]), l)
        kv = _memkv(mem, row(g_mem[l]), w_xkv, l)
        x2 = _xattn(x2.reshape(b, s, d), kv, w_xq, w_xo, row(g_x_pre[l]), row(g_x_post[l]), l).reshape(tok, d)
        x2 = _mlp(x2, w_up, w_down, row(g_mlp_pre[l]), row(g_mlp_post[l]), l)
    return x2.reshape(b, s, d)
```

```python
import functools
import math

import jax
import jax.numpy as jnp
from jax import lax
from jax.experimental import pallas as pl
from jax.experimental.pallas import tpu as pltpu

F32 = jnp.float32
BF16 = jnp.bfloat16

DEPTH = 2
RET_HEADS = 4
RET_DK = 128
RET_CHUNK = 128
RET_WIDTH = RET_HEADS * RET_DK
S5_WIDTH = 512
S5_GROUP = 16
S5_GROUPS = 32
S5_STATE = 64
S5_NSTATE = S5_GROUPS * S5_STATE
DIFF_HEADS = 8
DIFF_DV = 128
DIFF_DQK = 64
REL_BUCKETS = 32
REL_MAX_DIST = 128
X_HEADS = 4
X_HEAD_DIM = 128
X_WIDTH = X_HEADS * X_HEAD_DIM
EPS = 1e-6
LOG2E = math.log2(math.e)
IN_WIDTH = 5632
COL_RET_Q, COL_RET_K, COL_RET_V, COL_RET_G, COL_S5_U = 0, 1, 2, 3, 4
COL_DIFF_Q, COL_DIFF_K, COL_DIFF_V = 20, 28, 36

LANES = 128
SUBLANES = 8
MXU_TILE = 256

VMEM_LIMIT = 56 * 1024 * 1024
INPROJ_TM = 512
OUTPROJ_TM = 512
XATTN_TQ = 1024
MEMKV_TM = 1024
MLP_TM = 512
MLP_TF = 2048
RET_TBLK = 512
S5_SCAN_LANES = 512
S5_TSTEPS = 32
ATT_TILE = MXU_TILE

NT = (((1,), (1,)), ((), ()))
TN = (((0,), (0,)), ((), ()))


def _cparams(*sem):
    return pltpu.CompilerParams(dimension_semantics=sem, vmem_limit_bytes=VMEM_LIMIT)


def _rms(x, g):
    ms = jnp.mean(x * x, axis=-1, keepdims=True)
    return x * lax.rsqrt(ms + EPS) * g


def _dot(a, b):
    return jnp.dot(a, b, preferred_element_type=F32)


def _inproj_kernel(x_ref, g_ref, w_ref, o_ref):
    o_ref[...] = _dot(_rms(x_ref[...], g_ref[...]).astype(BF16), w_ref[...]).astype(o_ref.dtype)


def _inproj(x2, g, w, l):
    tok, d = x2.shape
    n = w.shape[2]
    tm = min(INPROJ_TM, tok)
    return pl.pallas_call(
        _inproj_kernel,
        grid=(tok // tm,),
        in_specs=[
            pl.BlockSpec((tm, d), lambda i: (i, 0)),
            pl.BlockSpec((1, d), lambda i: (0, 0)),
            pl.BlockSpec((None, d, n), lambda i: (l, 0, 0), pipeline_mode=pl.Buffered(1)),
        ],
        out_specs=pl.BlockSpec((tm, n), lambda i: (i, 0)),
        out_shape=jax.ShapeDtypeStruct((tok, n), BF16),
        compiler_params=_cparams("parallel"),
        name="inproj",
    )(x2, g, w)


def _rot_kernel(pos_ref, cos_ref, sin_ref):
    hs = pos_ref.shape[0] // 2
    half = RET_DK // 2
    lane = lax.broadcasted_iota(jnp.int32, (1, LANES), 1)
    lo = lane < half
    j = jnp.where(lo, lane, lane - half).astype(F32)
    inv = jnp.exp(j * (-math.log(10000.0) / half))
    pos = pos_ref[...].astype(F32)
    ang = jnp.where(lo, pos[:hs], pos[hs:]) * inv
    c = jnp.cos(ang)
    s = jnp.sin(ang)
    c_sw = pltpu.roll(c, half, 1)
    s_sw = pltpu.roll(s, half, 1)
    cos_ref[:hs] = jnp.where(lo, c, c_sw)
    cos_ref[hs:] = jnp.where(lo, c_sw, c)
    sin_ref[:hs] = jnp.where(lo, -s, s_sw)
    sin_ref[hs:] = jnp.where(lo, -s_sw, s)


def _rot_tables(positions):
    b, s = positions.shape
    out = jax.ShapeDtypeStruct((b, s, LANES), F32)
    return pl.pallas_call(
        _rot_kernel,
        grid=(b,),
        in_specs=[pl.BlockSpec((None, s, 1), lambda i: (i, 0, 0))],
        out_specs=[pl.BlockSpec((None, s, LANES), lambda i: (i, 0, 0))] * 2,
        out_shape=[out, out],
        compiler_params=_cparams("parallel"),
        name="rot_tables",
    )(positions.reshape(b, s, 1))


def _ret_kernel(q_ref, k_ref, v_ref, g_ref, cos_ref, sin_ref, dec_ref, xi_ref, zeta_ref, gc_ref, o_ref, r_ref):
    c_len = RET_CHUNK
    scale = RET_DK ** -0.5

    @pl.when(pl.program_id(1) == 0)
    def _():
        r_ref[...] = jnp.zeros_like(r_ref)

    n_c = q_ref.shape[0] // c_len
    stage = []
    for c in range(n_c):
        rows = slice(c * c_len, (c + 1) * c_len)
        cs = cos_ref[rows, :]
        sn = sin_ref[rows, :]
        for h in range(RET_HEADS):
            cols = slice(h * RET_DK, (h + 1) * RET_DK)
            q = q_ref[rows, cols].astype(F32)
            k = k_ref[rows, cols].astype(F32)
            qr = q * cs + pltpu.roll(q, RET_DK // 2, 1) * sn
            kr = (k * cs + pltpu.roll(k, RET_DK // 2, 1) * sn) * scale
            vb = v_ref[rows, cols]
            inner = lax.dot_general(qr.astype(BF16), kr.astype(BF16), NT, preferred_element_type=F32) * dec_ref[h]
            d_r = lax.dot_general((kr * zeta_ref[h]).astype(BF16), vb, TN, preferred_element_type=F32)
            lhs = jnp.concatenate([inner.astype(BF16), (qr * xi_ref[h]).astype(BF16)], axis=1)
            stage.append((lhs, vb, d_r))
    r_states = [r_ref[h] for h in range(RET_HEADS)]
    for c in range(n_c):
        rows = slice(c * c_len, (c + 1) * c_len)
        for h in range(RET_HEADS):
            cols = slice(h * RET_DK, (h + 1) * RET_DK)
            lhs, vb, d_r = stage[c * RET_HEADS + h]
            o = _dot(lhs, jnp.concatenate([vb, r_states[h].astype(BF16)], axis=0))
            r_states[h] = gc_ref[h] * r_states[h] + d_r
            o = o * lax.rsqrt(jnp.mean(o * o, axis=-1, keepdims=True) + EPS)
            gate = g_ref[rows, cols].astype(F32)
            o_ref[rows, cols] = (o * (gate * jax.nn.sigmoid(gate))).astype(o_ref.dtype)
    for h in range(RET_HEADS):
        r_ref[h] = r_states[h]


def _ret_consts():
    h, c = RET_HEADS, RET_CHUNK
    log_g = jnp.log(1.0 - 2.0 ** (-5.0 - jnp.arange(h, dtype=F32)))
    idx = jnp.arange(c, dtype=F32)
    dist = idx[:, None] - idx[None, :]
    decay = jnp.where(dist >= 0, jnp.exp(jnp.maximum(dist, 0.0)[None] * log_g[:, None, None]), 0.0)
    xi = jnp.exp((idx + 1.0)[None, :] * log_g[:, None])
    zeta = jnp.exp((c - 1.0 - idx)[None, :] * log_g[:, None])
    g_chunk = jnp.exp(c * log_g)
    bc = lambda t: jnp.broadcast_to(t[:, :, None], (h, c, LANES))
    gcb = jnp.broadcast_to(g_chunk[:, None, None], (h, c, LANES))
    return decay, bc(xi), bc(zeta), gcb


def _retention(proj3, cos2, sin2):
    b, s, _ = proj3.shape
    tb = min(RET_TBLK, s)
    decay, xi, zeta, gcb = _ret_consts()
    tok_blk = lambda col: pl.BlockSpec((None, tb, RET_WIDTH), lambda i, t: (i, t, col))
    pos_blk = pl.BlockSpec((None, tb, LANES), lambda i, t: (i, t, 0))
    const = pl.BlockSpec((RET_HEADS, RET_CHUNK, LANES), lambda i, t: (0, 0, 0))
    return pl.pallas_call(
        _ret_kernel,
        grid=(b, s // tb),
        in_specs=[tok_blk(COL_RET_Q), tok_blk(COL_RET_K), tok_blk(COL_RET_V), tok_blk(COL_RET_G),
                  pos_blk, pos_blk, const, const, const, const],
        out_specs=pl.BlockSpec((None, tb, RET_WIDTH), lambda i, t: (i, t, 0)),
        out_shape=jax.ShapeDtypeStruct((b, s, RET_WIDTH), BF16),
        scratch_shapes=[pltpu.VMEM((RET_HEADS, RET_DK, RET_DK), F32)],
        compiler_params=_cparams("parallel", "arbitrary"),
        name="retention",
    )(proj3, proj3, proj3, proj3, cos2, sin2, decay, xi, zeta, gcb)


def _s5prep_kernel(lr_ref, li_ref, ldt_ref, br_ref, bi_ref, ar_ref, ai_ref, bbr_ref, bbi_ref):
    lr = jnp.minimum(lr_ref[...], -1e-4)
    li = li_ref[...]
    dt = jnp.exp(ldt_ref[...])
    mag = jnp.exp(lr * dt)
    ar = mag * jnp.cos(li * dt)
    ai = mag * jnp.sin(li * dt)
    nr, ni = ar - 1.0, ai
    den = lr * lr + li * li
    fr = (nr * lr + ni * li) / den
    fi = (ni * lr - nr * li) / den
    br = br_ref[...]
    bi = bi_ref[...]
    bbr_ref[...] = fr[None] * br - fi[None] * bi
    bbi_ref[...] = fr[None] * bi + fi[None] * br
    ar_ref[...] = ar
    ai_ref[...] = ai


def _s5_prep(lam_re, lam_im, log_dt, b_re, b_im):
    g, p = lam_re.shape
    ab = jax.ShapeDtypeStruct((g, p), F32)
    bb = jax.ShapeDtypeStruct((S5_GROUP, g, p), F32)
    return pl.pallas_call(
        _s5prep_kernel,
        out_shape=[ab, ab, bb, bb],
        name="s5_prep",
    )(lam_re, lam_im, log_dt.reshape(g, 1), b_re.transpose(2, 0, 1), b_im.transpose(2, 0, 1))


def _s5_kernel(u_ref, perm_ref, perm_t_ref, bcat_ref, cre_ref, cim_ref, ar_ref, ai_ref, dskip_ref, wglu_ref, o_ref,
               bu_ref, st_ref):
    n = S5_NSTATE
    nb, t_steps, _ = u_ref.shape

    @pl.when(pl.program_id(0) == 0)
    def _():
        st_ref[...] = jnp.zeros_like(st_ref)

    u = _dot(perm_ref[...], u_ref[...].reshape(nb * t_steps, S5_WIDTH)).astype(BF16)
    hw = S5_WIDTH // 2
    hn = n // 2
    for part in range(2):
        for half in range(2):
            cols = slice(part * n + half * hn, part * n + (half + 1) * hn)
            bu_ref[:, cols] = _dot(u[:, half * hw:(half + 1) * hw], bcat_ref[half * hw:(half + 1) * hw, cols])

    for lo in range(0, n, S5_SCAN_LANES):
        sl_r = slice(lo, lo + S5_SCAN_LANES)
        sl_i = slice(n + lo, n + lo + S5_SCAN_LANES)
        a_r = jnp.broadcast_to(ar_ref[:, sl_r], (nb, S5_SCAN_LANES))
        a_i = jnp.broadcast_to(ai_ref[:, sl_r], (nb, S5_SCAN_LANES))

        xr = st_ref[:, sl_r]
        xi = st_ref[:, sl_i]
        for t in range(t_steps):
            rs = slice(t * nb, (t + 1) * nb)
            xr, xi = (a_r * xr - a_i * xi + bu_ref[rs, sl_r],
                      a_r * xi + a_i * xr + bu_ref[rs, sl_i])
            bu_ref[rs, sl_r] = xr
            bu_ref[rs, sl_i] = xi
        st_ref[:, sl_r] = xr
        st_ref[:, sl_i] = xi

    ys = []
    for half in range(2):
        xr = bu_ref[:, half * hn:(half + 1) * hn].astype(BF16)
        xi = bu_ref[:, n + half * hn:n + (half + 1) * hn].astype(BF16)
        rows = slice(half * hn, (half + 1) * hn)
        cols = slice(half * hw, (half + 1) * hw)
        ys.append(_dot(xr, cre_ref[rows, cols]) - _dot(xi, cim_ref[rows, cols]))
    y = jnp.concatenate(ys, axis=1) + dskip_ref[...] * u.astype(F32)
    gl = jax.nn.gelu(y)
    y_tb = (gl * jax.nn.sigmoid(_dot(gl.astype(BF16), wglu_ref[...]))).astype(BF16)
    o_ref[...] = _dot(perm_t_ref[...], y_tb).astype(o_ref.dtype).reshape(nb, t_steps, S5_WIDTH)


def _s5(proj3, lam_re, lam_im, log_dt, b_re, b_im, c_re, c_im, d_skip, w_glu):
    b, s, _ = proj3.shape
    g, hh = S5_GROUPS, S5_GROUP
    n = S5_NSTATE
    a_r, a_i, bb_r, bb_i = _s5_prep(lam_re, lam_im, log_dt, b_re, b_im)
    eye = jnp.eye(g, dtype=F32)
    blk_in = lambda t: jnp.einsum('hgp,gk->ghkp', t, eye).reshape(g * hh, n)
    blk_out = lambda t: jnp.einsum('ghp,gk->gpkh', t, eye).reshape(n, g * hh)
    bcat = jnp.concatenate([blk_in(bb_r), blk_in(bb_i)], axis=1).astype(BF16)
    cre = blk_out(c_re).astype(BF16)
    cim = blk_out(c_im).astype(BF16)
    ts = min(S5_TSTEPS, s)
    rows = b * ts
    r = jnp.arange(rows)
    perm = (r[:, None] % b * ts + r[:, None] // b == r[None, :]).astype(BF16)
    full = lambda shape: pl.BlockSpec(shape, lambda t: (0,) * len(shape))
    return pl.pallas_call(
        _s5_kernel,
        grid=(s // ts,),
        in_specs=[pl.BlockSpec((b, ts, S5_WIDTH), lambda t: (0, t, COL_S5_U)),
                  full((rows, rows)), full((rows, rows)),
                  full((S5_WIDTH, 2 * n)), full((n, S5_WIDTH)), full((n, S5_WIDTH)),
                  full((1, n)), full((1, n)), full((1, S5_WIDTH)), full((S5_WIDTH, S5_WIDTH))],
        out_specs=pl.BlockSpec((b, ts, S5_WIDTH), lambda t: (0, t, 0)),
        out_shape=jax.ShapeDtypeStruct((b, s, S5_WIDTH), BF16),
        scratch_shapes=[pltpu.VMEM((rows, 2 * n), F32), pltpu.VMEM((b, 2 * n), F32)],
        compiler_params=_cparams("arbitrary"),
        name="s5",
    )(proj3, perm, perm.T, bcat, cre, cim, a_r.reshape(1, n), a_i.reshape(1, n), d_skip.reshape(1, S5_WIDTH),
      w_glu.astype(BF16))


def _bias_kernel(tbl_ref, o_ref):
    h = pl.program_id(0)
    t = ATT_TILE
    r = lax.broadcasted_iota(jnp.int32, (t, t), 0)
    c = lax.broadcasted_iota(jnp.int32, (t, t), 1)
    max_exact = REL_BUCKETS // 2
    far = tbl_ref[REL_BUCKETS - 1, h]
    for d in range(2):
        n = jnp.maximum(r - c + t * d, 0)
        large = max_exact + (jnp.log(jnp.maximum(n, 1).astype(F32) / max_exact)
                             / math.log(REL_MAX_DIST / max_exact) * (REL_BUCKETS - max_exact)).astype(jnp.int32)
        large = jnp.minimum(large, REL_BUCKETS - 1)
        bucket = jnp.where(n < max_exact, n, large)
        acc = jnp.zeros((t, t), F32)
        for bk in range(REL_BUCKETS):
            acc = jnp.where(bucket == bk, tbl_ref[bk, h], acc)
        o_ref[d] = (acc - far) * LOG2E


def _bias_blocks(rel_bias):
    t = ATT_TILE
    return pl.pallas_call(
        _bias_kernel,
        grid=(DIFF_HEADS,),
        in_specs=[pl.BlockSpec(memory_space=pltpu.SMEM)],
        out_specs=pl.BlockSpec((None, 2, t, t), lambda h: (h, 0, 0, 0)),
        out_shape=jax.ShapeDtypeStruct((DIFF_HEADS, 2, t, t), F32),
        compiler_params=_cparams("arbitrary"),
        name="t5_bias_blocks",
    )(rel_bias)


DIFF_HEADS_PER_STEP = 2


def _diff_kernel(q_ref, k_ref, v_ref, bias_ref, lq1_ref, lk1_ref, lq2_ref, lk2_ref, gs_ref, o_ref, s_ref, *, lam_init):
    t = ATT_TILE
    n_tiles = q_ref.shape[0] // t
    lam = (jnp.exp(jnp.sum(lq1_ref[...] * lk1_ref[...], axis=-1, keepdims=True))
           - jnp.exp(jnp.sum(lq2_ref[...] * lk2_ref[...], axis=-1, keepdims=True)) + lam_init)
    lane_q = lax.broadcasted_iota(jnp.int32, (t, LANES), 1)
    col = lax.broadcasted_iota(jnp.int32, (2 * t, t), 1)
    row = lax.broadcasted_iota(jnp.int32, (2 * t, t), 0)
    causal2 = col <= jnp.where(row < t, row, row - t)
    gs = gs_ref[...] * (1.0 - lam_init)

    for i in list(range(1, n_tiles, 2)) + list(range((n_tiles - 1) // 2 * 2, -1, -2)):
        for hh in range(DIFF_HEADS_PER_STEP):
            hc = slice(hh * LANES, (hh + 1) * LANES)
            q = q_ref[i * t:(i + 1) * t, hc].astype(F32) * (DIFF_DQK ** -0.5 * LOG2E)
            q2 = jnp.concatenate([jnp.where(lane_q < DIFF_DQK, q, 0.0), jnp.where(lane_q >= DIFF_DQK, q, 0.0)],
                                 axis=0).astype(BF16)
            m_acc = None
            for j in range(i + 1):
                s = lax.dot_general(q2, k_ref[j * t:(j + 1) * t, hc], NT, preferred_element_type=F32)
                if j == i:
                    b0 = bias_ref[hh, 0]
                    s = jnp.where(causal2, s + jnp.concatenate([b0, b0], axis=0), -jnp.inf)
                elif j == i - 1:
                    b1 = bias_ref[hh, 1]
                    s = s + jnp.concatenate([b1, b1], axis=0)
                s_ref[hh, j] = s
                mj = jnp.maximum(s[:, :LANES], s[:, LANES:])
                m_acc = mj if m_acc is None else jnp.maximum(m_acc, mj)
            m = jnp.max(m_acc, axis=-1, keepdims=True)
            l_acc = jnp.zeros((2 * t, LANES), F32)
            o2 = jnp.zeros((2 * t, DIFF_DV), F32)
            for j in range(i + 1):
                e = jnp.exp2(s_ref[hh, j] - m)
                l_acc = l_acc + (e[:, :LANES] + e[:, LANES:])
                o2 = o2 + _dot(e.astype(BF16), v_ref[j * t:(j + 1) * t, hc])
            l_sum = jnp.sum(l_acc, axis=-1, keepdims=True)
            o = o2[:t] * (1.0 / l_sum[:t]) - o2[t:] * (lam / l_sum[t:])
            o = o * lax.rsqrt(jnp.mean(o * o, axis=-1, keepdims=True) + EPS) * gs
            o_ref[i * t:(i + 1) * t, hc] = o.astype(o_ref.dtype)


def _diff_attn(proj3, bias_blocks, lq1, lk1, lq2, lk2, g_subln, lam_init):
    b, s, _ = proj3.shape
    t = ATT_TILE
    hp = DIFF_HEADS_PER_STEP
    w = hp * LANES
    tok_blk = lambda col: pl.BlockSpec((None, s, w), lambda i, h: (i, 0, col // hp + h))
    vec = lambda n: pl.BlockSpec((1, n), lambda i, h: (0, 0))
    return pl.pallas_call(
        functools.partial(_diff_kernel, lam_init=lam_init),
        grid=(b, DIFF_HEADS // hp),
        in_specs=[tok_blk(COL_DIFF_Q), tok_blk(COL_DIFF_K), tok_blk(COL_DIFF_V),
                  pl.BlockSpec((hp, 2, t, t), lambda i, h: (h, 0, 0, 0)),
                  vec(DIFF_DQK), vec(DIFF_DQK), vec(DIFF_DQK), vec(DIFF_DQK), vec(DIFF_DV)],
        out_specs=pl.BlockSpec((None, s, w), lambda i, h: (i, 0, h)),
        out_shape=jax.ShapeDtypeStruct((b, s, DIFF_HEADS * DIFF_DV), BF16),
        scratch_shapes=[pltpu.VMEM((hp, s // t, 2 * t, t), F32)],
        compiler_params=_cparams("parallel", "parallel"),
        name="diff_attn",
    )(proj3, proj3, proj3, bias_blocks,
      lq1.reshape(1, -1), lk1.reshape(1, -1), lq2.reshape(1, -1), lk2.reshape(1, -1), g_subln.reshape(1, -1))


def _outproj_kernel(yr_ref, ys_ref, yd_ref, x_ref, w_ref, g_ref, o_ref):
    mixed = jnp.concatenate([yr_ref[...], ys_ref[...], yd_ref[...]], axis=1)
    o_ref[...] = x_ref[...] + _rms(_dot(mixed, w_ref[...]), g_ref[...])


def _outproj(y_ret, y_s5, y_diff, x2, w, g, l):
    tok, d = x2.shape
    tm = min(OUTPROJ_TM, tok)
    row = lambda n: pl.BlockSpec((tm, n), lambda i: (i, 0))
    return pl.pallas_call(
        _outproj_kernel,
        grid=(tok // tm,),
        in_specs=[row(y_ret.shape[1]), row(y_s5.shape[1]), row(y_diff.shape[1]), row(d),
                  pl.BlockSpec((None, d, d), lambda i: (l, 0, 0)), pl.BlockSpec((1, d), lambda i: (0, 0))],
        out_specs=row(d),
        out_shape=jax.ShapeDtypeStruct((tok, d), F32),
        compiler_params=_cparams("parallel"),
        name="outproj",
    )(y_ret, y_s5, y_diff, x2, w, g)


def _memkv_kernel(m_ref, g_ref, w_ref, o_ref):
    o_ref[...] = _dot(_rms(m_ref[...], g_ref[...]).astype(BF16), w_ref[...]).astype(o_ref.dtype)


def _memkv(mem, g, w, l):
    b, m, d = mem.shape
    n = w.shape[2]
    rows = b * m
    tm = min(MEMKV_TM, rows)
    out = pl.pallas_call(
        _memkv_kernel,
        grid=(rows // tm,),
        in_specs=[pl.BlockSpec((tm, d), lambda i: (i, 0)),
                  pl.BlockSpec((1, d), lambda i: (0, 0)), pl.BlockSpec((None, d, n), lambda i: (l, 0, 0))],
        out_specs=pl.BlockSpec((tm, n), lambda i: (i, 0)),
        out_shape=jax.ShapeDtypeStruct((rows, n), BF16),
        compiler_params=_cparams("parallel"),
        name="mem_kv",
    )(mem.reshape(rows, d), g, w)
    return out.reshape(b, m, n)


def _xattn_kernel(x_ref, kv_ref, wq_ref, wo_ref, gpre_ref, gpost_ref, o_ref):
    x = x_ref[...]
    hn = _rms(x, gpre_ref[...]).astype(BF16)
    q = (_dot(hn, wq_ref[...]) * (X_HEAD_DIM ** -0.5)).astype(BF16)
    outs = []
    for hd in range(X_HEADS):
        lo = hd * X_HEAD_DIM
        kh = kv_ref[:, lo:lo + X_HEAD_DIM]
        vh = kv_ref[:, X_WIDTH + lo:X_WIDTH + lo + X_HEAD_DIM]
        s = lax.dot_general(q[:, lo:lo + X_HEAD_DIM], kh, NT, preferred_element_type=F32)
        e = jnp.exp(s - jnp.max(s, axis=-1, keepdims=True))
        l_sum = jnp.sum(e, axis=-1, keepdims=True)
        outs.append(_dot(e.astype(BF16), vh) / l_sum)
    o = jnp.concatenate(outs, axis=-1).astype(BF16)
    o_ref[...] = x + _rms(_dot(o, wo_ref[...]), gpost_ref[...])


def _xattn(x3, kv, wq, wo, gpre, gpost, l):
    b, s, d = x3.shape
    m = kv.shape[1]
    tq = min(XATTN_TQ, s)
    full = lambda shape: pl.BlockSpec(shape, lambda i, t: (0,) * len(shape))
    return pl.pallas_call(
        _xattn_kernel,
        grid=(b, s // tq),
        in_specs=[pl.BlockSpec((None, tq, d), lambda i, t: (i, t, 0)),
                  pl.BlockSpec((None, m, 2 * X_WIDTH), lambda i, t: (i, 0, 0)),
                  pl.BlockSpec((None, d, X_WIDTH), lambda i, t: (l, 0, 0)),
                  pl.BlockSpec((None, X_WIDTH, d), lambda i, t: (l, 0, 0)), full((1, d)), full((1, d))],
        out_specs=pl.BlockSpec((None, tq, d), lambda i, t: (i, t, 0)),
        out_shape=jax.ShapeDtypeStruct((b, s, d), F32),
        compiler_params=_cparams("parallel", "parallel"),
        name="xattn",
    )(x3, kv, wq, wo, gpre, gpost)


def _mlp_kernel(x_ref, wu_ref, wd_ref, gpre_ref, gpost_ref, o_ref, h_ref):
    j = pl.program_id(1)
    last = pl.num_programs(1) - 1

    def partial_out(h):
        u = jnp.maximum(_dot(h, wu_ref[...]), 0.0)
        return _dot((u * u).astype(BF16), wd_ref[...])

    @pl.when(j == 0)
    def _():
        h = _rms(x_ref[...], gpre_ref[...]).astype(BF16)
        h_ref[...] = h
        o_ref[...] = partial_out(h)

    @pl.when((j > 0) & (j < last))
    def _():
        o_ref[...] += partial_out(h_ref[...])

    @pl.when(j == last)
    def _():
        o_ref[...] = x_ref[...] + _rms(o_ref[...] + partial_out(h_ref[...]), gpost_ref[...])


def _mlp(x2, wu, wd, gpre, gpost, l):
    tok, d = x2.shape
    ff = wu.shape[2]
    tm = min(MLP_TM, tok)
    tf = MLP_TF
    return pl.pallas_call(
        _mlp_kernel,
        grid=(tok // tm, ff // tf),
        in_specs=[pl.BlockSpec((tm, d), lambda i, j: (i, 0)),
                  pl.BlockSpec((None, d, tf), lambda i, j: (l, 0, j)),
                  pl.BlockSpec((None, tf, d), lambda i, j: (l, j, 0)),
                  pl.BlockSpec((1, d), lambda i, j: (0, 0)), pl.BlockSpec((1, d), lambda i, j: (0, 0))],
        out_specs=pl.BlockSpec((tm, d), lambda i, j: (i, 0)),
        out_shape=jax.ShapeDtypeStruct((tok, d), F32),
        scratch_shapes=[pltpu.VMEM((tm, d), BF16)],
        compiler_params=_cparams("parallel", "arbitrary"),
        name="mlp",
    )(x2, wu, wd, gpre, gpost)


def kernel(x, mem, positions, rel_bias, w_in, w_out, lam_re, lam_im, log_dt, b_re, b_im, c_re, c_im, d_skip, w_glu, lam_q1, lam_k1, lam_q2, lam_k2, g_subln, w_xq, w_xkv, w_xo, w_up, w_down, g_mix_pre, g_mix_post, g_mem, g_x_pre, g_x_post, g_mlp_pre, g_mlp_post):
    b, s, d = x.shape
    tok = b * s
    row = lambda g: g.reshape(1, -1)
    cos2, sin2 = _rot_tables(positions)
    bias_blocks = _bias_blocks(rel_bias)
    w_in, w_out, w_xq, w_xkv, w_xo, w_up, w_down = (
        w.astype(BF16) for w in (w_in, w_out, w_xq, w_xkv, w_xo, w_up, w_down))
    x2 = x.reshape(tok, d)
    for l in range(DEPTH):
        lam_init = 0.8 - 0.6 * math.exp(-0.3 * l)
        proj3 = _inproj(x2, row(g_mix_pre[l]), w_in, l).reshape(b, s, IN_WIDTH)
        y_ret = _retention(proj3, cos2, sin2)
        y_s5 = _s5(proj3, lam_re[l], lam_im[l], log_dt[l], b_re[l], b_im[l], c_re[l], c_im[l], d_skip[l], w_glu[l])
        y_diff = _diff_attn(proj3, bias_blocks, lam_q1[l], lam_k1[l], lam_q2[l], lam_k2[l], g_subln[l], lam_init)
        x2 = _outproj(y_ret.reshape(tok, -1), y_s5.reshape(tok, -1), y_diff.reshape(tok, -1), x2,
                      w_out, row(g_mix_post[l]), l)
        kv = _memkv(mem, row(g_mem[l]), w_xkv, l)
        x2 = _xattn(x2.reshape(b, s, d), kv, w_xq, w_xo, row(g_x_pre[l]), row(g_x_post[l]), l).reshape(tok, d)
        x2 = _mlp(x2, w_up, w_down, row(g_mlp_pre[l]), row(g_mlp_post[l]), l)
    return x2.reshape(b, s, d)
```

```python
import functools
import math

import jax
import jax.numpy as jnp
from jax import lax
from jax.experimental import pallas as pl
from jax.experimental.pallas import tpu as pltpu

F32 = jnp.float32
BF16 = jnp.bfloat16

DEPTH = 2
RET_HEADS = 4
RET_DK = 128
RET_CHUNK = 128
RET_WIDTH = RET_HEADS * RET_DK
S5_WIDTH = 512
S5_GROUP = 16
S5_GROUPS = 32
S5_STATE = 64
S5_NSTATE = S5_GROUPS * S5_STATE
DIFF_HEADS = 8
DIFF_DV = 128
DIFF_DQK = 64
REL_BUCKETS = 32
REL_MAX_DIST = 128
X_HEADS = 4
X_HEAD_DIM = 128
X_WIDTH = X_HEADS * X_HEAD_DIM
EPS = 1e-6
LOG2E = math.log2(math.e)
IN_WIDTH = 5632
COL_RET_Q, COL_RET_K, COL_RET_V, COL_RET_G, COL_S5_U = 0, 1, 2, 3, 4
COL_DIFF_Q, COL_DIFF_K, COL_DIFF_V = 20, 28, 36

LANES = 128
SUBLANES = 8
MXU_TILE = 256

VMEM_LIMIT = 56 * 1024 * 1024
INPROJ_TM = 512
OUTPROJ_TM = 512
XATTN_TQ = 1024
MEMKV_TM = 1024
MLP_TM = 512
MLP_TF = 2048
RET_TBLK = 512
S5_SCAN_LANES = 512
S5_TSTEPS = 32
ATT_TILE = MXU_TILE

NT = (((1,), (1,)), ((), ()))
TN = (((0,), (0,)), ((), ()))


def _cparams(*sem):
    return pltpu.CompilerParams(dimension_semantics=sem, vmem_limit_bytes=VMEM_LIMIT)


def _rms(x, g):
    ms = jnp.mean(x * x, axis=-1, keepdims=True)
    return x * lax.rsqrt(ms + EPS) * g


def _dot(a, b):
    return jnp.dot(a, b, preferred_element_type=F32)


def _inproj_kernel(x_ref, g_ref, w_ref, o_ref):
    o_ref[...] = _dot(_rms(x_ref[...], g_ref[...]).astype(BF16), w_ref[...]).astype(o_ref.dtype)


def _inproj(x2, g, w, l):
    tok, d = x2.shape
    n = w.shape[2]
    tm = min(INPROJ_TM, tok)
    return pl.pallas_call(
        _inproj_kernel,
        grid=(tok // tm,),
        in_specs=[
            pl.BlockSpec((tm, d), lambda i: (i, 0)),
            pl.BlockSpec((1, d), lambda i: (0, 0)),
            pl.BlockSpec((None, d, n), lambda i: (l, 0, 0), pipeline_mode=pl.Buffered(1)),
        ],
        out_specs=pl.BlockSpec((tm, n), lambda i: (i, 0)),
        out_shape=jax.ShapeDtypeStruct((tok, n), BF16),
        compiler_params=_cparams("parallel"),
        name="inproj",
    )(x2, g, w)


def _rot_kernel(pos_ref, cos_ref, sin_ref):
    hs = pos_ref.shape[0] // 2
    half = RET_DK // 2
    lane = lax.broadcasted_iota(jnp.int32, (1, LANES), 1)
    lo = lane < half
    j = jnp.where(lo, lane, lane - half).astype(F32)
    inv = jnp.exp(j * (-math.log(10000.0) / half))
    pos = pos_ref[...].astype(F32)
    ang = jnp.where(lo, pos[:hs], pos[hs:]) * inv
    c = jnp.cos(ang)
    s = jnp.sin(ang)
    c_sw = pltpu.roll(c, half, 1)
    s_sw = pltpu.roll(s, half, 1)
    cos_ref[:hs] = jnp.where(lo, c, c_sw)
    cos_ref[hs:] = jnp.where(lo, c_sw, c)
    sin_ref[:hs] = jnp.where(lo, -s, s_sw)
    sin_ref[hs:] = jnp.where(lo, -s_sw, s)


def _rot_tables(positions):
    b, s = positions.shape
    out = jax.ShapeDtypeStruct((b, s, LANES), F32)
    return pl.pallas_call(
        _rot_kernel,
        grid=(b,),
        in_specs=[pl.BlockSpec((None, s, 1), lambda i: (i, 0, 0))],
        out_specs=[pl.BlockSpec((None, s, LANES), lambda i: (i, 0, 0))] * 2,
        out_shape=[out, out],
        compiler_params=_cparams("parallel"),
        name="rot_tables",
    )(positions.reshape(b, s, 1))


def _ret_kernel(q_ref, k_ref, v_ref, g_ref, cos_ref, sin_ref, dec_ref, xi_ref, zeta_ref, gc_ref, o_ref, r_ref):
    c_len = RET_CHUNK
    scale = RET_DK ** -0.5

    @pl.when(pl.program_id(1) == 0)
    def _():
        r_ref[...] = jnp.zeros_like(r_ref)

    n_c = q_ref.shape[0] // c_len
    stage = []
    for c in range(n_c):
        rows = slice(c * c_len, (c + 1) * c_len)
        cs = cos_ref[rows, :]
        sn = sin_ref[rows, :]
        for h in range(RET_HEADS):
            cols = slice(h * RET_DK, (h + 1) * RET_DK)
            q = q_ref[rows, cols].astype(F32)
            k = k_ref[rows, cols].astype(F32)
            qr = q * cs + pltpu.roll(q, RET_DK // 2, 1) * sn
            kr = (k * cs + pltpu.roll(k, RET_DK // 2, 1) * sn) * scale
            vb = v_ref[rows, cols]
            inner = lax.dot_general(qr.astype(BF16), kr.astype(BF16), NT, preferred_element_type=F32) * dec_ref[h]
            d_r = lax.dot_general((kr * zeta_ref[h]).astype(BF16), vb, TN, preferred_element_type=F32)
            lhs = jnp.concatenate([inner.astype(BF16), (qr * xi_ref[h]).astype(BF16)], axis=1)
            stage.append((lhs, vb, d_r))
    r_states = [r_ref[h] for h in range(RET_HEADS)]
    for c in range(n_c):
        rows = slice(c * c_len, (c + 1) * c_len)
        for h in range(RET_HEADS):
            cols = slice(h * RET_DK, (h + 1) * RET_DK)
            lhs, vb, d_r = stage[c * RET_HEADS + h]
            o = _dot(lhs, jnp.concatenate([vb, r_states[h].astype(BF16)], axis=0))
            r_states[h] = gc_ref[h] * r_states[h] + d_r
            o = o * lax.rsqrt(jnp.mean(o * o, axis=-1, keepdims=True) + EPS)
            gate = g_ref[rows, cols].astype(F32)
            o_ref[rows, cols] = (o * (gate * jax.nn.sigmoid(gate))).astype(o_ref.dtype)
    for h in range(RET_HEADS):
        r_ref[h] = r_states[h]


def _ret_consts():
    h, c = RET_HEADS, RET_CHUNK
    log_g = jnp.log(1.0 - 2.0 ** (-5.0 - jnp.arange(h, dtype=F32)))
    idx = jnp.arange(c, dtype=F32)
    dist = idx[:, None] - idx[None, :]
    decay = jnp.where(dist >= 0, jnp.exp(jnp.maximum(dist, 0.0)[None] * log_g[:, None, None]), 0.0)
    xi = jnp.exp((idx + 1.0)[None, :] * log_g[:, None])
    zeta = jnp.exp((c - 1.0 - idx)[None, :] * log_g[:, None])
    g_chunk = jnp.exp(c * log_g)
    bc = lambda t: jnp.broadcast_to(t[:, :, None], (h, c, LANES))
    gcb = jnp.broadcast_to(g_chunk[:, None, None], (h, c, LANES))
    return decay, bc(xi), bc(zeta), gcb


def _retention(proj3, cos2, sin2):
    b, s, _ = proj3.shape
    tb = min(RET_TBLK, s)
    decay, xi, zeta, gcb = _ret_consts()
    tok_blk = lambda col: pl.BlockSpec((None, tb, RET_WIDTH), lambda i, t: (i, t, col))
    pos_blk = pl.BlockSpec((None, tb, LANES), lambda i, t: (i, t, 0))
    const = pl.BlockSpec((RET_HEADS, RET_CHUNK, LANES), lambda i, t: (0, 0, 0))
    return pl.pallas_call(
        _ret_kernel,
        grid=(b, s // tb),
        in_specs=[tok_blk(COL_RET_Q), tok_blk(COL_RET_K), tok_blk(COL_RET_V), tok_blk(COL_RET_G),
                  pos_blk, pos_blk, const, const, const, const],
        out_specs=pl.BlockSpec((None, tb, RET_WIDTH), lambda i, t: (i, t, 0)),
        out_shape=jax.ShapeDtypeStruct((b, s, RET_WIDTH), BF16),
        scratch_shapes=[pltpu.VMEM((RET_HEADS, RET_DK, RET_DK), F32)],
        compiler_params=_cparams("parallel", "arbitrary"),
        name="retention",
    )(proj3, proj3, proj3, proj3, cos2, sin2, decay, xi, zeta, gcb)


def _s5prep_kernel(lr_ref, li_ref, ldt_ref, br_ref, bi_ref, ar_ref, ai_ref, bbr_ref, bbi_ref):
    lr = jnp.minimum(lr_ref[...], -1e-4)
    li = li_ref[...]
    dt = jnp.exp(ldt_ref[...])
    mag = jnp.exp(lr * dt)
    ar = mag * jnp.cos(li * dt)
    ai = mag * jnp.sin(li * dt)
    nr, ni = ar - 1.0, ai
    den = lr * lr + li * li
    fr = (nr * lr + ni * li) / den
    fi = (ni * lr - nr * li) / den
    br = br_ref[...]
    bi = bi_ref[...]
    bbr_ref[...] = fr[None] * br - fi[None] * bi
    bbi_ref[...] = fr[None] * bi + fi[None] * br
    ar_ref[...] = ar
    ai_ref[...] = ai


def _s5_prep(lam_re, lam_im, log_dt, b_re, b_im):
    g, p = lam_re.shape
    ab = jax.ShapeDtypeStruct((g, p), F32)
    bb = jax.ShapeDtypeStruct((S5_GROUP, g, p), F32)
    return pl.pallas_call(
        _s5prep_kernel,
        out_shape=[ab, ab, bb, bb],
        name="s5_prep",
    )(lam_re, lam_im, log_dt.reshape(g, 1), b_re.transpose(2, 0, 1), b_im.transpose(2, 0, 1))


def _s5_kernel(u_ref, perm_ref, perm_t_ref, bcat_ref, cre_ref, cim_ref, ar_ref, ai_ref, dskip_ref, wglu_ref, o_ref,
               bu_ref, st_ref):
    n = S5_NSTATE
    nb, t_steps, _ = u_ref.shape

    @pl.when(pl.program_id(0) == 0)
    def _():
        st_ref[...] = jnp.zeros_like(st_ref)

    u = _dot(perm_ref[...], u_ref[...].reshape(nb * t_steps, S5_WIDTH)).astype(BF16)
    hw = S5_WIDTH // 2
    hn = n // 2
    for part in range(2):
        for half in range(2):
            cols = slice(part * n + half * hn, part * n + (half + 1) * hn)
            bu_ref[:, cols] = _dot(u[:, half * hw:(half + 1) * hw], bcat_ref[half * hw:(half + 1) * hw, cols])

    for lo in range(0, n, S5_SCAN_LANES):
        sl_r = slice(lo, lo + S5_SCAN_LANES)
        sl_i = slice(n + lo, n + lo + S5_SCAN_LANES)
        a_r = jnp.broadcast_to(ar_ref[:, sl_r], (nb, S5_SCAN_LANES))
        a_i = jnp.broadcast_to(ai_ref[:, sl_r], (nb, S5_SCAN_LANES))

        xr = st_ref[:, sl_r]
        xi = st_ref[:, sl_i]
        for t in range(t_steps):
            rs = slice(t * nb, (t + 1) * nb)
            xr, xi = (a_r * xr - a_i * xi + bu_ref[rs, sl_r],
                      a_r * xi + a_i * xr + bu_ref[rs, sl_i])
            bu_ref[rs, sl_r] = xr
            bu_ref[rs, sl_i] = xi
        st_ref[:, sl_r] = xr
        st_ref[:, sl_i] = xi

    ys = []
    for half in range(2):
        xr = bu_ref[:, half * hn:(half + 1) * hn].astype(BF16)
        xi = bu_ref[:, n + half * hn:n + (half + 1) * hn].astype(BF16)
        rows = slice(half * hn, (half + 1) * hn)
        cols = slice(half * hw, (half + 1) * hw)
        ys.append(_dot(xr, cre_ref[rows, cols]) - _dot(xi, cim_ref[rows, cols]))
    y = jnp.concatenate(ys, axis=1) + dskip_ref[...] * u.astype(F32)
    gl = jax.nn.gelu(y)
    y_tb = (gl * jax.nn.sigmoid(_dot(gl.astype(BF16), wglu_ref[...]))).astype(BF16)
    o_ref[...] = _dot(perm_t_ref[...], y_tb).astype(o_ref.dtype).reshape(nb, t_steps, S5_WIDTH)


def _s5(proj3, lam_re, lam_im, log_dt, b_re, b_im, c_re, c_im, d_skip, w_glu):
    b, s, _ = proj3.shape
    g, hh = S5_GROUPS, S5_GROUP
    n = S5_NSTATE
    a_r, a_i, bb_r, bb_i = _s5_prep(lam_re, lam_im, log_dt, b_re, b_im)
    eye = jnp.eye(g, dtype=F32)
    blk_in = lambda t: jnp.einsum('hgp,gk->ghkp', t, eye).reshape(g * hh, n)
    blk_out = lambda t: jnp.einsum('ghp,gk->gpkh', t, eye).reshape(n, g * hh)
    bcat = jnp.concatenate([blk_in(bb_r), blk_in(bb_i)], axis=1).astype(BF16)
    cre = blk_out(c_re).astype(BF16)
    cim = blk_out(c_im).astype(BF16)
    ts = min(S5_TSTEPS, s)
    rows = b * ts
    r = jnp.arange(rows)
    perm = (r[:, None] % b * ts + r[:, None] // b == r[None, :]).astype(BF16)
    full = lambda shape: pl.BlockSpec(shape, lambda t: (0,) * len(shape))
    return pl.pallas_call(
        _s5_kernel,
        grid=(s // ts,),
        in_specs=[pl.BlockSpec((b, ts, S5_WIDTH), lambda t: (0, t, COL_S5_U)),
                  full((rows, rows)), full((rows, rows)),
                  full((S5_WIDTH, 2 * n)), full((n, S5_WIDTH)), full((n, S5_WIDTH)),
                  full((1, n)), full((1, n)), full((1, S5_WIDTH)), full((S5_WIDTH, S5_WIDTH))],
        out_specs=pl.BlockSpec((b, ts, S5_WIDTH), lambda t: (0, t, 0)),
        out_shape=jax.ShapeDtypeStruct((b, s, S5_WIDTH), BF16),
        scratch_shapes=[pltpu.VMEM((rows, 2 * n), F32), pltpu.VMEM((b, 2 * n), F32)],
        compiler_params=_cparams("arbitrary"),
        name="s5",
    )(proj3, perm, perm.T, bcat, cre, cim, a_r.reshape(1, n), a_i.reshape(1, n), d_skip.reshape(1, S5_WIDTH),
      w_glu.astype(BF16))


def _bias_kernel(tbl_ref, o_ref):
    h = pl.program_id(0)
    t = ATT_TILE
    r = lax.broadcasted_iota(jnp.int32, (t, t), 0)
    c = lax.broadcasted_iota(jnp.int32, (t, t), 1)
    max_exact = REL_BUCKETS // 2
    far = tbl_ref[REL_BUCKETS - 1, h]
    for d in range(2):
        n = jnp.maximum(r - c + t * d, 0)
        large = max_exact + (jnp.log(jnp.maximum(n, 1).astype(F32) / max_exact)
                             / math.log(REL_MAX_DIST / max_exact) * (REL_BUCKETS - max_exact)).astype(jnp.int32)
        large = jnp.minimum(large, REL_BUCKETS - 1)
        bucket = jnp.where(n < max_exact, n, large)
        acc = jnp.zeros((t, t), F32)
        for bk in range(REL_BUCKETS):
            acc = jnp.where(bucket == bk, tbl_ref[bk, h], acc)
        o_ref[d] = (acc - far) * LOG2E


def _bias_blocks(rel_bias):
    t = ATT_TILE
    return pl.pallas_call(
        _bias_kernel,
        grid=(DIFF_HEADS,),
        in_specs=[pl.BlockSpec(memory_space=pltpu.SMEM)],
        out_specs=pl.BlockSpec((None, 2, t, t), lambda h: (h, 0, 0, 0)),
        out_shape=jax.ShapeDtypeStruct((DIFF_HEADS, 2, t, t), F32),
        compiler_params=_cparams("arbitrary"),
        name="t5_bias_blocks",
    )(rel_bias)


DIFF_HEADS_PER_STEP = 2


def _diff_kernel(q_ref, k_ref, v_ref, bias_ref, lq1_ref, lk1_ref, lq2_ref, lk2_ref, gs_ref, o_ref, s_ref, *, lam_init):
    t = ATT_TILE
    n_tiles = q_ref.shape[0] // t
    lam = (jnp.exp(jnp.sum(lq1_ref[...] * lk1_ref[...], axis=-1, keepdims=True))
           - jnp.exp(jnp.sum(lq2_ref[...] * lk2_ref[...], axis=-1, keepdims=True)) + lam_init)
    lane_q = lax.broadcasted_iota(jnp.int32, (t, LANES), 1)
    col = lax.broadcasted_iota(jnp.int32, (2 * t, t), 1)
    row = lax.broadcasted_iota(jnp.int32, (2 * t, t), 0)
    causal2 = col <= jnp.where(row < t, row, row - t)
    gs = gs_ref[...] * (1.0 - lam_init)

    for i in list(range(1, n_tiles, 2)) + list(range((n_tiles - 1) // 2 * 2, -1, -2)):
        for hh in range(DIFF_HEADS_PER_STEP):
            hc = slice(hh * LANES, (hh + 1) * LANES)
            q = q_ref[i * t:(i + 1) * t, hc].astype(F32) * (DIFF_DQK ** -0.5 * LOG2E)
            q2 = jnp.concatenate([jnp.where(lane_q < DIFF_DQK, q, 0.0), jnp.where(lane_q >= DIFF_DQK, q, 0.0)],
                                 axis=0).astype(BF16)
            m_acc = None
            for j in range(i + 1):
                s = lax.dot_general(q2, k_ref[j * t:(j + 1) * t, hc], NT, preferred_element_type=F32)
                if j == i:
                    b0 = bias_ref[hh, 0]
                    s = jnp.where(causal2, s + jnp.concatenate([b0, b0], axis=0), -jnp.inf)
                elif j == i - 1:
                    b1 = bias_ref[hh, 1]
                    s = s + jnp.concatenate([b1, b1], axis=0)
                s_ref[hh, j] = s
                mj = jnp.maximum(s[:, :LANES], s[:, LANES:])
                m_acc = mj if m_acc is None else jnp.maximum(m_acc, mj)
            m = jnp.max(m_acc, axis=-1, keepdims=True)
            l_acc = jnp.zeros((2 * t, LANES), F32)
            o2 = jnp.zeros((2 * t, DIFF_DV), F32)
            for j in range(i + 1):
                e = jnp.exp2(s_ref[hh, j] - m)
                l_acc = l_acc + (e[:, :LANES] + e[:, LANES:])
                o2 = o2 + _dot(e.astype(BF16), v_ref[j * t:(j + 1) * t, hc])
            l_sum = jnp.sum(l_acc, axis=-1, keepdims=True)
            o = o2[:t] * (1.0 / l_sum[:t]) - o2[t:] * (lam / l_sum[t:])
            o = o * lax.rsqrt(jnp.mean(o * o, axis=-1, keepdims=True) + EPS) * gs
            o_ref[i * t:(i + 1) * t, hc] = o.astype(o_ref.dtype)


def _diff_attn(proj3, bias_blocks, lq1, lk1, lq2, lk2, g_subln, lam_init):
    b, s, _ = proj3.shape
    t = ATT_TILE
    hp = DIFF_HEADS_PER_STEP
    w = hp * LANES
    tok_blk = lambda col: pl.BlockSpec((None, s, w), lambda i, h: (i, 0, col // hp + h))
    vec = lambda n: pl.BlockSpec((1, n), lambda i, h: (0, 0))
    return pl.pallas_call(
        functools.partial(_diff_kernel, lam_init=lam_init),
        grid=(b, DIFF_HEADS // hp),
        in_specs=[tok_blk(COL_DIFF_Q), tok_blk(COL_DIFF_K), tok_blk(COL_DIFF_V),
                  pl.BlockSpec((hp, 2, t, t), lambda i, h: (h, 0, 0, 0)),
                  vec(DIFF_DQK), vec(DIFF_DQK), vec(DIFF_DQK), vec(DIFF_DQK), vec(DIFF_DV)],
        out_specs=pl.BlockSpec((None, s, w), lambda i, h: (i, 0, h)),
        out_shape=jax.ShapeDtypeStruct((b, s, DIFF_HEADS * DIFF_DV), BF16),
        scratch_shapes=[pltpu.VMEM((hp, s // t, 2 * t, t), F32)],
        compiler_params=_cparams("parallel", "parallel"),
        name="diff_attn",
    )(proj3, proj3, proj3, bias_blocks,
      lq1.reshape(1, -1), lk1.reshape(1, -1), lq2.reshape(1, -1), lk2.reshape(1, -1), g_subln.reshape(1, -1))


def _outproj_kernel(yr_ref, ys_ref, yd_ref, x_ref, w_ref, g_ref, o_ref):
    mixed = jnp.concatenate([yr_ref[...], ys_ref[...], yd_ref[...]], axis=1)
    o_ref[...] = x_ref[...] + _rms(_dot(mixed, w_ref[...]), g_ref[...])


def _outproj(y_ret, y_s5, y_diff, x2, w, g, l):
    tok, d = x2.shape
    tm = min(OUTPROJ_TM, tok)
    row = lambda n: pl.BlockSpec((tm, n), lambda i: (i, 0))
    return pl.pallas_call(
        _outproj_kernel,
        grid=(tok // tm,),
        in_specs=[row(y_ret.shape[1]), row(y_s5.shape[1]), row(y_diff.shape[1]), row(d),
                  pl.BlockSpec((None, d, d), lambda i: (l, 0, 0)), pl.BlockSpec((1, d), lambda i: (0, 0))],
        out_specs=row(d),
        out_shape=jax.ShapeDtypeStruct((tok, d), F32),
        compiler_params=_cparams("parallel"),
        name="outproj",
    )(y_ret, y_s5, y_diff, x2, w, g)


def _memkv_kernel(m_ref, g_ref, w_ref, o_ref):
    o_ref[...] = _dot(_rms(m_ref[...], g_ref[...]).astype(BF16), w_ref[...]).astype(o_ref.dtype)


def _memkv(mem, g, w, l):
    b, m, d = mem.shape
    n = w.shape[2]
    rows = b * m
    tm = min(MEMKV_TM, rows)
    out = pl.pallas_call(
        _memkv_kernel,
        grid=(rows // tm,),
        in_specs=[pl.BlockSpec((tm, d), lambda i: (i, 0)),
                  pl.BlockSpec((1, d), lambda i: (0, 0)), pl.BlockSpec((None, d, n), lambda i: (l, 0, 0))],
        out_specs=pl.BlockSpec((tm, n), lambda i: (i, 0)),
        out_shape=jax.ShapeDtypeStruct((rows, n), BF16),
        compiler_params=_cparams("parallel"),
        name="mem_kv",
    )(mem.reshape(rows, d), g, w)
    return out.reshape(b, m, n)


def _xattn_kernel(x_ref, kv_ref, wq_ref, wo_ref, gpre_ref, gpost_ref, o_ref):
    x = x_ref[...]
    hn = _rms(x, gpre_ref[...]).astype(BF16)
    q = (_dot(hn, wq_ref[...]) * (X_HEAD_DIM ** -0.5)).astype(BF16)
    outs = []
    for hd in range(X_HEADS):
        lo = hd * X_HEAD_DIM
        kh = kv_ref[:, lo:lo + X_HEAD_DIM]
        vh = kv_ref[:, X_WIDTH + lo:X_WIDTH + lo + X_HEAD_DIM]
        s = lax.dot_general(q[:, lo:lo + X_HEAD_DIM], kh, NT, preferred_element_type=F32)
        e = jnp.exp(s - jnp.max(s, axis=-1, keepdims=True))
        l_sum = jnp.sum(e, axis=-1, keepdims=True)
        outs.append(_dot(e.astype(BF16), vh) / l_sum)
    o = jnp.concatenate(outs, axis=-1).astype(BF16)
    o_ref[...] = x + _rms(_dot(o, wo_ref[...]), gpost_ref[...])


def _xattn(x3, kv, wq, wo, gpre, gpost, l):
    b, s, d = x3.shape
    m = kv.shape[1]
    tq = min(XATTN_TQ, s)
    full = lambda shape: pl.BlockSpec(shape, lambda i, t: (0,) * len(shape))
    return pl.pallas_call(
        _xattn_kernel,
        grid=(b, s // tq),
        in_specs=[pl.BlockSpec((None, tq, d), lambda i, t: (i, t, 0)),
                  pl.BlockSpec((None, m, 2 * X_WIDTH), lambda i, t: (i, 0, 0)),
                  pl.BlockSpec((None, d, X_WIDTH), lambda i, t: (l, 0, 0)),
                  pl.BlockSpec((None, X_WIDTH, d), lambda i, t: (l, 0, 0)), full((1, d)), full((1, d))],
        out_specs=pl.BlockSpec((None, tq, d), lambda i, t: (i, t, 0)),
        out_shape=jax.ShapeDtypeStruct((b, s, d), F32),
        compiler_params=_cparams("parallel", "parallel"),
        name="xattn",
    )(x3, kv, wq, wo, gpre, gpost)


MIXX_TQ = 512


def _mixx_kernel(yr_ref, ys_ref, yd_ref, x_ref, wout_ref, gmix_ref, kv_ref, wq_ref, wo_ref, gpre_ref, gpost_ref, o_ref):
    mixed = jnp.concatenate([yr_ref[...], ys_ref[...], yd_ref[...]], axis=1)
    x = x_ref[...] + _rms(_dot(mixed, wout_ref[...]), gmix_ref[...])
    hn = _rms(x, gpre_ref[...]).astype(BF16)
    q = (_dot(hn, wq_ref[...]) * (X_HEAD_DIM ** -0.5)).astype(BF16)
    outs = []
    for hd in range(X_HEADS):
        lo = hd * X_HEAD_DIM
        kh = kv_ref[:, lo:lo + X_HEAD_DIM]
        vh = kv_ref[:, X_WIDTH + lo:X_WIDTH + lo + X_HEAD_DIM]
        s = lax.dot_general(q[:, lo:lo + X_HEAD_DIM], kh, NT, preferred_element_type=F32)
        e = jnp.exp(s - jnp.max(s, axis=-1, keepdims=True))
        l_sum = jnp.sum(e, axis=-1, keepdims=True)
        outs.append(_dot(e.astype(BF16), vh) / l_sum)
    o = jnp.concatenate(outs, axis=-1).astype(BF16)
    o_ref[...] = x + _rms(_dot(o, wo_ref[...]), gpost_ref[...])


def _mixx(y_ret, y_s5, y_diff, x3, w_out, g_mix, kv, wq, wo, gpre, gpost, l):
    b, s, d = x3.shape
    m = kv.shape[1]
    tq = min(MIXX_TQ, s)
    tok = lambda n: pl.BlockSpec((None, tq, n), lambda i, t: (i, t, 0))
    full = lambda shape: pl.BlockSpec(shape, lambda i, t: (0,) * len(shape))
    layer = lambda r, c: pl.BlockSpec((None, r, c), lambda i, t: (l, 0, 0), pipeline_mode=pl.Buffered(1))
    return pl.pallas_call(
        _mixx_kernel,
        grid=(b, s // tq),
        in_specs=[tok(y_ret.shape[2]), tok(y_s5.shape[2]), tok(y_diff.shape[2]), tok(d),
                  layer(d, d), full((1, d)),
                  pl.BlockSpec((None, m, 2 * X_WIDTH), lambda i, t: (i, 0, 0)),
                  layer(d, X_WIDTH), layer(X_WIDTH, d), full((1, d)), full((1, d))],
        out_specs=tok(d),
        out_shape=jax.ShapeDtypeStruct((b, s, d), F32),
        compiler_params=_cparams("parallel", "parallel"),
        name="mix_xattn",
    )(y_ret, y_s5, y_diff, x3, w_out, g_mix, kv, wq, wo, gpre, gpost)


def _mlp_kernel(x_ref, wu_ref, wd_ref, gpre_ref, gpost_ref, o_ref, h_ref):
    j = pl.program_id(1)
    last = pl.num_programs(1) - 1

    def partial_out(h):
        u = jnp.maximum(_dot(h, wu_ref[...]), 0.0)
        return _dot((u * u).astype(BF16), wd_ref[...])

    @pl.when(j == 0)
    def _():
        h = _rms(x_ref[...], gpre_ref[...]).astype(BF16)
        h_ref[...] = h
        o_ref[...] = partial_out(h)

    @pl.when((j > 0) & (j < last))
    def _():
        o_ref[...] += partial_out(h_ref[...])

    @pl.when(j == last)
    def _():
        o_ref[...] = x_ref[...] + _rms(o_ref[...] + partial_out(h_ref[...]), gpost_ref[...])


def _mlp(x2, wu, wd, gpre, gpost, l):
    tok, d = x2.shape
    ff = wu.shape[2]
    tm = min(MLP_TM, tok)
    tf = MLP_TF
    return pl.pallas_call(
        _mlp_kernel,
        grid=(tok // tm, ff // tf),
        in_specs=[pl.BlockSpec((tm, d), lambda i, j: (i, 0)),
                  pl.BlockSpec((None, d, tf), lambda i, j: (l, 0, j)),
                  pl.BlockSpec((None, tf, d), lambda i, j: (l, j, 0)),
                  pl.BlockSpec((1, d), lambda i, j: (0, 0)), pl.BlockSpec((1, d), lambda i, j: (0, 0))],
        out_specs=pl.BlockSpec((tm, d), lambda i, j: (i, 0)),
        out_shape=jax.ShapeDtypeStruct((tok, d), F32),
        scratch_shapes=[pltpu.VMEM((tm, d), BF16)],
        compiler_params=_cparams("parallel", "arbitrary"),
        name="mlp",
    )(x2, wu, wd, gpre, gpost)


def kernel(x, mem, positions, rel_bias, w_in, w_out, lam_re, lam_im, log_dt, b_re, b_im, c_re, c_im, d_skip, w_glu, lam_q1, lam_k1, lam_q2, lam_k2, g_subln, w_xq, w_xkv, w_xo, w_up, w_down, g_mix_pre, g_mix_post, g_mem, g_x_pre, g_x_post, g_mlp_pre, g_mlp_post):
    b, s, d = x.shape
    tok = b * s
    row = lambda g: g.reshape(1, -1)
    cos2, sin2 = _rot_tables(positions)
    bias_blocks = _bias_blocks(rel_bias)
    w_in, w_out, w_xq, w_xkv, w_xo, w_up, w_down = (
        w.astype(BF16) for w in (w_in, w_out, w_xq, w_xkv, w_xo, w_up, w_down))
    x2 = x.reshape(tok, d)
    for l in range(DEPTH):
        lam_init = 0.8 - 0.6 * math.exp(-0.3 * l)
        proj3 = _inproj(x2, row(g_mix_pre[l]), w_in, l).reshape(b, s, IN_WIDTH)
        y_ret = _retention(proj3, cos2, sin2)
        y_s5 = _s5(proj3, lam_re[l], lam_im[l], log_dt[l], b_re[l], b_im[l], c_re[l], c_im[l], d_skip[l], w_glu[l])
        y_diff = _diff_attn(proj3, bias_blocks, lam_q1[l], lam_k1[l], lam_q2[l], lam_k2[l], g_subln[l], lam_init)
        kv = _memkv(mem, row(g_mem[l]), w_xkv, l)
        x2 = _mixx(y_ret, y_s5, y_diff, x2.reshape(b, s, d), w_out, row(g_mix_post[l]), kv, w_xq, w_xo,
                   row(g_x_pre[l]), row(g_x_post[l]), l).reshape(tok, d)
        x2 = _mlp(x2, w_up, w_down, row(g_mlp_pre[l]), row(g_mlp_post[l]), l)
    return x2.reshape(b, s, d)
```
